```python
import jax, jax.numpy as jnp
from jax import lax
import numpy as np

D_MODEL = 1024
BATCH = 16
SEQ = 256
DEPTH = 4
DEC_BATCH = 8
DEC_SEQ = 2048
PAST_LEN = 256

GRID_W = 64
MIX_GLA = 512
MIX_CONV = 512
GLA_HEADS = 4
GLA_DK = 64
GLA_DV = MIX_GLA // GLA_HEADS
GLA_KEY = GLA_HEADS * GLA_DK
GATE_RANK = 16
GATE_TAU = 16.0
CHUNK = 64
CONV_W = 31
N_EXPERTS = 16
EC_FACTOR = 2
D_EXPERT = 1024
EPS = 1e-6
IN_COLS = 2 * GLA_KEY + 2 * MIX_GLA + 2 * GATE_RANK + 2 * MIX_CONV

kernel_name = 'hymba_gla_conformer_ec_diffusion_step'


def _rmsnorm(x, g):
    xf = x.astype(jnp.float32)
    y = xf * lax.rsqrt(jnp.mean(xf * xf, axis=-1, keepdims=True) + EPS)
    return (y * g.astype(jnp.float32)).astype(x.dtype)


def _layernorm(x, g, b):
    xf = x.astype(jnp.float32)
    xc = xf - jnp.mean(xf, axis=-1, keepdims=True)
    y = xc * lax.rsqrt(jnp.mean(xc * xc, axis=-1, keepdims=True) + EPS)
    return (y * g.astype(jnp.float32) + b.astype(jnp.float32)).astype(x.dtype)


def _gla_scan(q, k, v, g, s0):
    b_, h_, t_, dk = q.shape
    dv = v.shape[-1]
    n = t_ // CHUNK
    q = q.reshape(b_, h_, n, CHUNK, dk)
    k = k.reshape(b_, h_, n, CHUNK, dk)
    v = v.reshape(b_, h_, n, CHUNK, dv)
    cum = jnp.cumsum(g.reshape(b_, h_, n, CHUNK, dk), axis=3)
    cum_last = cum[:, :, :, -1:, :]
    q_dec = q * jnp.exp(cum)
    k_inv = k * jnp.exp(-cum)
    lower = jnp.tril(jnp.ones((CHUNK, CHUNK), dtype=bool))
    scores = jnp.where(lower, jnp.einsum('bhnik,bhnjk->bhnij', q_dec, k_inv), 0.0)
    o_intra = jnp.einsum('bhnij,bhnjv->bhniv', scores, v)
    k_end = k * jnp.exp(cum_last - cum)
    upd = jnp.einsum('bhnck,bhncv->bhnkv', k_end, v)
    decay = jnp.exp(cum_last[:, :, :, 0, :])

    def step(s, inp):
        d, u = inp
        return d[..., None] * s + u, s

    s_fin, s_prev = lax.scan(step, s0, (jnp.moveaxis(decay, 2, 0), jnp.moveaxis(upd, 2, 0)))
    s_prev = jnp.moveaxis(s_prev, 0, 2)
    o_inter = jnp.einsum('bhnck,bhnkv->bhncv', q_dec, s_prev)
    return (o_intra + o_inter).reshape(b_, h_, t_, dv), s_fin


def _depthwise_conv(u, w, b):
    y = lax.conv_general_dilated(u, w[:, None, :], window_strides=(1,),
                                 padding=[(CONV_W // 2, CONV_W // 2)],
                                 dimension_numbers=('NWC', 'WIO', 'NWC'),
                                 feature_group_count=u.shape[-1])
    return y + b


def _mixer(h, grid, s0f, s0b, w_in, w_decay_f, b_decay_f, w_decay_b, b_decay_b,
           gla_norm_g, conv_w, conv_b, conv_ln_g, conv_ln_b, w_out):
    bsz, t, _ = h.shape
    proj = jnp.einsum('btd,dc->btc', h, w_in)
    o1 = GLA_KEY
    o2 = o1 + GLA_KEY
    o3 = o2 + MIX_GLA
    o4 = o3 + MIX_GLA
    o5 = o4 + GATE_RANK
    o6 = o5 + GATE_RANK
    o7 = o6 + MIX_CONV
    q, k, v, r, zf, zb, ga, gg = jnp.split(proj, [o1, o2, o3, o4, o5, o6, o7], axis=-1)

    def heads(a):
        return a.reshape(bsz, t, GLA_HEADS, -1).transpose(0, 2, 1, 3).astype(jnp.float32)

    gf = jax.nn.log_sigmoid((zf @ w_decay_f + b_decay_f).astype(jnp.float32)) / GATE_TAU
    gb = jax.nn.log_sigmoid((zb @ w_decay_b + b_decay_b).astype(jnp.float32)) / GATE_TAU
    qh = heads(q) * (GLA_DK ** -0.5)
    kh = heads(k)
    vh = heads(v)
    of, sf = _gla_scan(qh, kh, vh, heads(gf), s0f)

    def flip(a):
        return jnp.flip(a, axis=2)

    ob, sb = _gla_scan(flip(qh), flip(kh), flip(vh), flip(heads(gb)), s0b)
    o = (of + flip(ob)).transpose(0, 2, 1, 3).astype(h.dtype)
    o = _rmsnorm(o, gla_norm_g) * jax.nn.silu(r.reshape(bsz, t, GLA_HEADS, GLA_DV))
    o = o.reshape(bsz, t, MIX_GLA)

    u = ga * jax.nn.sigmoid(gg)
    if grid:
        rows = t // GRID_W
        u = _depthwise_conv(u.reshape(bsz * rows, GRID_W, MIX_CONV), conv_w, conv_b)
        u = u.reshape(bsz, t, MIX_CONV)
    else:
        u = _depthwise_conv(u, conv_w, conv_b)
    u = jax.nn.silu(_layernorm(u, conv_ln_g, conv_ln_b))

    out = jnp.einsum('btc,cd->btd', jnp.concatenate([o, u], axis=-1), w_out)
    return out, sf, sb


def _ec_moe(h, w_router, w_e_gate, w_e_up, w_e_down):
    bsz, t, _ = h.shape
    cap = EC_FACTOR * t // N_EXPERTS
    logits = jnp.einsum('btd,de->bte', h, w_router).astype(jnp.float32)
    aff = jax.nn.softmax(logits, axis=-1)
    gate, idx = lax.top_k(jnp.swapaxes(aff, 1, 2), cap)
    bidx = jnp.arange(bsz)[:, None, None]
    xs = h[bidx, idx]
    a = jnp.einsum('becd,edf->becf', xs, w_e_gate)
    u = jnp.einsum('becd,edf->becf', xs, w_e_up)
    y = jnp.einsum('becf,efd->becd', jax.nn.silu(a) * u, w_e_down)
    y = y * gate[..., None].astype(h.dtype)
    return jnp.zeros_like(h).at[bidx, idx].add(y)


def _layer(x, cond, grid, s0f, s0b, norm_mix_g, norm_ffn_g, w_mod, b_mod, w_in,
           w_decay_f, b_decay_f, w_decay_b, b_decay_b, gla_norm_g, conv_w, conv_b,
           conv_ln_g, conv_ln_b, w_out, w_router, w_e_gate, w_e_up, w_e_down):
    mod = (jax.nn.silu(cond) @ w_mod + b_mod)[:, None, :]
    sh1, sc1, gt1, sh2, sc2, gt2 = jnp.split(mod, 6, axis=-1)
    h = _rmsnorm(x, norm_mix_g) * (1.0 + sc1) + sh1
    mix, sf, sb = _mixer(h, grid, s0f, s0b, w_in, w_decay_f, b_decay_f, w_decay_b, b_decay_b,
                         gla_norm_g, conv_w, conv_b, conv_ln_g, conv_ln_b, w_out)
    x = x + gt1 * mix
    h = _rmsnorm(x, norm_ffn_g) * (1.0 + sc2) + sh2
    x = x + gt2 * _ec_moe(h, w_router, w_e_gate, w_e_up, w_e_down)
    return x, sf, sb


def setup_inputs(seed: int = 0) -> dict:
    key = jax.random.key(seed)
    ks = jax.random.split(key, 28)

    def nrm(k, shape, s):
        return jax.random.normal(k, shape, jnp.float32) * s

    st_shape = (DEC_BATCH, DEPTH, GLA_HEADS, GLA_DK, GLA_DV)
    return {
        'x_prompt': nrm(ks[0], (BATCH, SEQ, D_MODEL), 1.0),
        'x_sample': nrm(ks[1], (DEC_BATCH, DEC_SEQ, D_MODEL), 1.0),
        'state_gla_fwd': nrm(ks[2], st_shape, 1.0),
        'state_gla_bwd': nrm(ks[3], st_shape, 1.0),
        'c': nrm(ks[4], (DEC_BATCH, D_MODEL), 1.0),
        'c_ctx': nrm(ks[5], (D_MODEL,), 1.0),
        'norm_mix_g': 1.0 + nrm(ks[6], (DEPTH, D_MODEL), 0.05),
        'norm_ffn_g': 1.0 + nrm(ks[7], (DEPTH, D_MODEL), 0.05),
        'norm_final_g': 1.0 + nrm(ks[8], (D_MODEL,), 0.05),
        'w_mod': nrm(ks[9], (DEPTH, D_MODEL, 6 * D_MODEL), 0.5 * D_MODEL ** -0.5),
        'b_mod': nrm(ks[10], (DEPTH, 6 * D_MODEL), 0.01),
        'w_in': nrm(ks[11], (DEPTH, D_MODEL, IN_COLS), D_MODEL ** -0.5),
        'w_decay_f': nrm(ks[12], (DEPTH, GATE_RANK, GLA_KEY), GATE_RANK ** -0.5),
        'b_decay_f': nrm(ks[13], (DEPTH, GLA_KEY), 0.1),
        'w_decay_b': nrm(ks[14], (DEPTH, GATE_RANK, GLA_KEY), GATE_RANK ** -0.5),
        'b_decay_b': nrm(ks[15], (DEPTH, GLA_KEY), 0.1),
        'gla_norm_g': 1.0 + nrm(ks[16], (DEPTH, GLA_DV), 0.05),
        'conv_w': nrm(ks[17], (DEPTH, CONV_W, MIX_CONV), CONV_W ** -0.5),
        'conv_b': nrm(ks[18], (DEPTH, MIX_CONV), 0.01),
        'conv_ln_g': 1.0 + nrm(ks[19], (DEPTH, MIX_CONV), 0.05),
        'conv_ln_b': nrm(ks[20], (DEPTH, MIX_CONV), 0.01),
        'w_out': nrm(ks[21], (DEPTH, MIX_GLA + MIX_CONV, D_MODEL), (MIX_GLA + MIX_CONV) ** -0.5),
        'w_router': nrm(ks[22], (DEPTH, D_MODEL, N_EXPERTS), D_MODEL ** -0.5),
        'w_e_gate': nrm(ks[23], (DEPTH, N_EXPERTS, D_MODEL, D_EXPERT), D_MODEL ** -0.5),
        'w_e_up': nrm(ks[24], (DEPTH, N_EXPERTS, D_MODEL, D_EXPERT), D_MODEL ** -0.5),
        'w_e_down': nrm(ks[25], (DEPTH, N_EXPERTS, D_EXPERT, D_MODEL), D_EXPERT ** -0.5),
    }


def reference(x_prompt, x_sample, state_gla_fwd, state_gla_bwd, c, c_ctx,
              norm_mix_g, norm_ffn_g, norm_final_g, w_mod, b_mod, w_in,
              w_decay_f, b_decay_f, w_decay_b, b_decay_b, gla_norm_g,
              conv_w, conv_b, conv_ln_g, conv_ln_b, w_out, w_router,
              w_e_gate, w_e_up, w_e_down):
    xp = x_prompt
    xs = x_sample
    zero_state = jnp.zeros((x_prompt.shape[0], GLA_HEADS, GLA_DK, GLA_DV), jnp.float32)
    ctx_cond = c_ctx[None, :]
    fwd_states = []
    bwd_states = []
    for l in range(DEPTH):
        lw = (norm_mix_g[l], norm_ffn_g[l], w_mod[l], b_mod[l], w_in[l],
              w_decay_f[l], b_decay_f[l], w_decay_b[l], b_decay_b[l], gla_norm_g[l],
              conv_w[l], conv_b[l], conv_ln_g[l], conv_ln_b[l], w_out[l],
              w_router[l], w_e_gate[l], w_e_up[l], w_e_down[l])
        xp, sf, sb = _layer(xp, ctx_cond, False, zero_state, zero_state, *lw)
        fwd_states.append(sf)
        bwd_states.append(sb)
        xs, _, _ = _layer(xs, c, True,
                          state_gla_fwd[:, l].astype(jnp.float32),
                          state_gla_bwd[:, l].astype(jnp.float32), *lw)
    y_prompt = _rmsnorm(xp, norm_final_g)
    y_sample = _rmsnorm(xs, norm_final_g)
    new_state_gla_fwd = jnp.stack(fwd_states, axis=1).astype(x_prompt.dtype)
    new_state_gla_bwd = jnp.stack(bwd_states, axis=1).astype(x_prompt.dtype)
    return (y_prompt, y_sample, new_state_gla_fwd, new_state_gla_bwd)
```

```python
import functools

import jax
import jax.numpy as jnp
from jax import lax
from jax.experimental import pallas as pl
from jax.experimental.pallas import tpu as pltpu

F32 = jnp.float32
BF16 = jnp.bfloat16
I32 = jnp.int32

D_MODEL = 1024
GRID_W = 64
MIX_GLA = 512
MIX_CONV = 512
GLA_HEADS = 4
GLA_DK = 64
GLA_DV = 128
GLA_KEY = GLA_HEADS * GLA_DK
GATE_RANK = 16
GATE_TAU = 16.0
CHUNK = 64
CONV_W = 31
CONV_HALF = CONV_W // 2
N_EXPERTS = 16
EC_FACTOR = 2
D_EXPERT = 1024
EPS = 1e-6

LANES = 128
SUBLANES = 8

PROJ_COLS = 2 * GLA_KEY + 2 * MIX_GLA + 2 * MIX_CONV + LANES
COLBLK = 512
Z_COLBLK = (2 * GLA_KEY + 2 * MIX_GLA + 2 * MIX_CONV) // LANES

VMEM_LIMIT = 56 * 1024 * 1024
CONV_PAD = 16
CONV_CH = 64


def _cparams():
    return pltpu.CompilerParams(vmem_limit_bytes=VMEM_LIMIT)


def _silu(x):
    return x * jax.nn.sigmoid(x)


def _rms(x):
    return x * lax.rsqrt(jnp.mean(x * x, axis=-1, keepdims=True) + EPS)


def _dot(a, b):
    return jnp.dot(a, b, preferred_element_type=F32)


def _mod_kernel(c_ref, w_ref, b_ref, o_ref):
    s = _silu(c_ref[...]).astype(BF16)
    o_ref[0] = _dot(s, w_ref[0].astype(BF16)) + b_ref[0]


def _modulation(cond, w_mod, b_mod):
    depth = w_mod.shape[0]
    n_out = w_mod.shape[2]
    tn = 1536
    rows = cond.shape[0]
    return pl.pallas_call(
        _mod_kernel,
        grid=(depth, n_out // tn),
        in_specs=[
            pl.BlockSpec((rows, D_MODEL), lambda l, j: (0, 0)),
            pl.BlockSpec((1, D_MODEL, tn), lambda l, j: (l, 0, j)),
            pl.BlockSpec((1, 1, tn), lambda l, j: (l, 0, j)),
        ],
        out_specs=pl.BlockSpec((1, rows, tn), lambda l, j: (l, 0, j)),
        out_shape=jax.ShapeDtypeStruct((depth, rows, n_out), F32),
        compiler_params=_cparams(),
        name="modulation",
    )(cond, w_mod, b_mod.reshape(depth, 1, n_out))


def _inproj_kernel(x_ref, mod_ref, g_ref, w_ref, o_ref):
    m = mod_ref[0]
    sh = m[:, 0:D_MODEL]
    sc = m[:, D_MODEL:2 * D_MODEL]
    h = (_rms(x_ref[...]) * g_ref[0]) * (1.0 + sc) + sh
    o_ref[...] = _dot(h.astype(BF16), w_ref[0])


def _inproj(x, modp, norm_g, w_in_b, layer, bsz, t):
    tm = min(t, 512)
    nt = t // tm
    return pl.pallas_call(
        _inproj_kernel,
        grid=(bsz, nt),
        in_specs=[
            pl.BlockSpec((tm, D_MODEL), lambda b, i: (b * nt + i, 0)),
            pl.BlockSpec((1, 1, 6 * D_MODEL), lambda b, i: (b, 0, 0)),
            pl.BlockSpec((1, 1, D_MODEL), lambda b, i: (layer, 0, 0)),
            pl.BlockSpec((1, D_MODEL, PROJ_COLS), lambda b, i: (layer, 0, 0)),
        ],
        out_specs=pl.BlockSpec((tm, PROJ_COLS), lambda b, i: (b * nt + i, 0)),
        out_shape=jax.ShapeDtypeStruct((bsz * t, PROJ_COLS), F32),
        compiler_params=_cparams(),
        name="inproj",
    )(x, modp, norm_g, w_in_b)


def _log_sigmoid(x):
    return jnp.minimum(x, 0.0) - jnp.log(1.0 + jnp.exp(-jnp.abs(x)))


def _head_stack(x, lane_head):
    return jnp.concatenate([jnp.where(lane_head == h, x, 0.0) for h in range(GLA_HEADS)], axis=0).astype(BF16)


def _gla_chunk(qk, v, g, tri, causal, lane_head, s_ref, d, last_row):
    g_hi = g.astype(BF16)
    g_lo = (g - g_hi.astype(F32)).astype(BF16)
    cum = _dot(tri, g_hi) + _dot(tri, g_lo)
    cl = cum[last_row:last_row + 1, :]
    q = qk[:, 0:GLA_KEY]
    k = qk[:, GLA_KEY:2 * GLA_KEY]
    q_dec = q * (jnp.exp(cum) * (GLA_DK ** -0.5))
    k_inv = k * jnp.exp(-cum)
    k_end = k_inv * jnp.exp(cl)
    tr = jnp.concatenate([k_end, cum], axis=0).T
    qs = _head_stack(q_dec, lane_head)
    ks = _head_stack(k_inv, lane_head)
    sc = lax.dot_general(qs, ks, (((1,), (1,)), ((), ())), preferred_element_type=F32)
    sc = (sc * causal).astype(BF16)
    vb = v.astype(BF16)
    vs = jnp.concatenate([vb[:, h * GLA_DV:(h + 1) * GLA_DV] for h in range(GLA_HEADS)], axis=0)
    s_prev = s_ref[d]
    o_st = _dot(jnp.concatenate([sc, qs], axis=1),
                jnp.concatenate([vs, s_prev.astype(BF16)], axis=0))
    upd_all = _dot(tr[:, 0:CHUNK].astype(BF16), vb)
    upd = jnp.concatenate([upd_all[h * GLA_DK:(h + 1) * GLA_DK, h * GLA_DV:(h + 1) * GLA_DV]
                           for h in range(GLA_HEADS)], axis=0)
    decay = jnp.exp(tr[:, CHUNK + last_row:CHUNK + last_row + 1])
    s_ref[d] = decay * s_prev + upd
    return jnp.concatenate([o_st[h * CHUNK:(h + 1) * CHUNK, :] for h in range(GLA_HEADS)], axis=1)


def _gla_kernel(*refs, t, zero_init, emit_state):
    qk_ref, v_ref, z_ref, wf_ref, bf_ref, wb_ref, bb_ref = refs[:7]
    pos = 7
    if not zero_init:
        s0f_ref, s0b_ref = refs[pos:pos + 2]
        pos += 2
    o_ref = refs[pos]
    pos += 1
    if emit_state:
        sf_ref, sb_ref = refs[pos:pos + 2]
        pos += 2
    gf_s, gb_s, s_ref, mask_s = refs[pos:pos + 4]

    z = z_ref[...].astype(BF16)
    gf_s[...] = _log_sigmoid(_dot(z, wf_ref[0]) + bf_ref[0]) * (1.0 / GATE_TAU)
    gb_s[...] = _log_sigmoid(_dot(z, wb_ref[0]) + bb_ref[0]) * (1.0 / GATE_TAU)
    if zero_init:
        s_ref[...] = jnp.zeros(s_ref.shape, F32)
    else:
        s_ref[0] = s0f_ref[0, 0].reshape(GLA_KEY, GLA_DV)
        s_ref[1] = s0b_ref[0, 0].reshape(GLA_KEY, GLA_DV)

    ri = lax.broadcasted_iota(I32, (CHUNK, CHUNK), 0)
    ci = lax.broadcasted_iota(I32, (CHUNK, CHUNK), 1)
    tri_low = jnp.where(ri >= ci, 1.0, 0.0).astype(BF16)
    tri_upp = jnp.where(ri <= ci, 1.0, 0.0).astype(BF16)
    rs = lax.broadcasted_iota(I32, (GLA_HEADS * CHUNK, GLA_HEADS * CHUNK), 0) % CHUNK
    cs = lax.broadcasted_iota(I32, (GLA_HEADS * CHUNK, GLA_HEADS * CHUNK), 1) % CHUNK
    mask_s[0] = jnp.where(rs >= cs, 1.0, 0.0)
    mask_s[1] = jnp.where(rs <= cs, 1.0, 0.0)
    lane_head = lax.broadcasted_iota(I32, (CHUNK, GLA_KEY), 1) // GLA_DK
    n = t // CHUNK

    def step(i, first):
        rf = pl.multiple_of(i * CHUNK, CHUNK)
        rb = pl.multiple_of((n - 1 - i) * CHUNK, CHUNK)
        o_f = _gla_chunk(qk_ref[pl.ds(rf, CHUNK), :], v_ref[pl.ds(rf, CHUNK), :],
                         gf_s[pl.ds(rf, CHUNK), :], tri_low, mask_s[0], lane_head, s_ref, 0, CHUNK - 1)
        o_b = _gla_chunk(qk_ref[pl.ds(rb, CHUNK), :], v_ref[pl.ds(rb, CHUNK), :],
                         gb_s[pl.ds(rb, CHUNK), :], tri_upp, mask_s[1], lane_head, s_ref, 1, 0)
        if first:
            o_ref[pl.ds(rf, CHUNK), :] = o_f
            o_ref[pl.ds(rb, CHUNK), :] = o_b
        else:
            o_ref[pl.ds(rf, CHUNK), :] += o_f
            o_ref[pl.ds(rb, CHUNK), :] += o_b

    def first_half(i, c):
        step(i, True)
        return c

    def second_half(i, c):
        step(i, False)
        return c

    lax.fori_loop(0, n // 2, first_half, 0, unroll=2)
    lax.fori_loop(n // 2, n, second_half, 0, unroll=2)
    if emit_state:
        sf_ref[0, 0] = s_ref[0].reshape(GLA_HEADS, GLA_DK, GLA_DV)
        sb_ref[0, 0] = s_ref[1].reshape(GLA_HEADS, GLA_DK, GLA_DV)


def _gla(proj, wdf, bdf, wdb, bdb, s0f, s0b, layer, bsz, t, emit_state):
    zero_init = s0f is None
    st_block = (1, 1, GLA_HEADS, GLA_DK, GLA_DV)
    in_specs = [
        pl.BlockSpec((t, COLBLK), lambda b: (b, 0)),
        pl.BlockSpec((t, COLBLK), lambda b: (b, 1)),
        pl.BlockSpec((t, LANES), lambda b: (b, Z_COLBLK)),
        pl.BlockSpec((1, LANES, GLA_KEY), lambda b: (layer, 0, 0)),
        pl.BlockSpec((1, 1, GLA_KEY), lambda b: (layer, 0, 0)),
        pl.BlockSpec((1, LANES, GLA_KEY), lambda b: (layer, 0, 0)),
        pl.BlockSpec((1, 1, GLA_KEY), lambda b: (layer, 0, 0)),
    ]
    args = [proj, proj, proj, wdf, bdf, wdb, bdb]
    if not zero_init:
        in_specs += [pl.BlockSpec(st_block, lambda b: (b, layer, 0, 0, 0))] * 2
        args += [s0f, s0b]
    out_specs = [pl.BlockSpec((t, MIX_GLA), lambda b: (b, 0))]
    out_shape = [jax.ShapeDtypeStruct((bsz * t, MIX_GLA), F32)]
    if emit_state:
        out_specs += [pl.BlockSpec((1, 1, GLA_HEADS, GLA_DK, GLA_DV), lambda b: (b, 0, 0, 0, 0))] * 2
        out_shape += [jax.ShapeDtypeStruct((bsz, 1, GLA_HEADS, GLA_DK, GLA_DV), F32)] * 2
    return pl.pallas_call(
        functools.partial(_gla_kernel, t=t, zero_init=zero_init, emit_state=emit_state),
        grid=(bsz,),
        in_specs=in_specs,
        out_specs=out_specs,
        out_shape=out_shape,
        scratch_shapes=[
            pltpu.VMEM((t, GLA_KEY), F32),
            pltpu.VMEM((t, GLA_KEY), F32),
            pltpu.VMEM((2, GLA_KEY, GLA_DV), F32),
            pltpu.VMEM((2, GLA_HEADS * CHUNK, GLA_HEADS * CHUNK), F32),
        ],
        compiler_params=_cparams(),
        name="gla",
    )(*args)


def _conv_kernel(ga_ref, gg_ref, w_ref, b_ref, lg_ref, lb_ref, o_ref, pad_ref, *, nrows, rlen):
    stride = rlen + 2 * CONV_PAD
    u = ga_ref[...] * jax.nn.sigmoid(gg_ref[...])
    zeros = jnp.zeros((CONV_PAD, MIX_CONV), F32)
    for r in range(nrows):
        base = r * stride
        pad_ref[base:base + CONV_PAD, :] = zeros
        pad_ref[base + CONV_PAD:base + CONV_PAD + rlen, :] = u[r * rlen:(r + 1) * rlen, :]
        pad_ref[base + CONV_PAD + rlen:base + stride, :] = zeros
    bias = b_ref[0]
    lg = lg_ref[0]
    lb = lb_ref[0]
    assert rlen == CONV_CH or nrows == 1
    pstep = stride if rlen == CONV_CH else CONV_CH
    off = CONV_PAD - CONV_HALF
    ngrp = CONV_CH // SUBLANES
    nshift = (CONV_CH + 2 * CONV_PAD) // SUBLANES - 1

    def body(c, carry):
        p0 = pl.multiple_of(c * pstep, SUBLANES)
        o0 = pl.multiple_of(c * CONV_CH, CONV_CH)
        for lt in range(MIX_CONV // LANES):
            ls = slice(lt * LANES, (lt + 1) * LANES)
            win = pad_ref[pl.ds(p0, CONV_CH + 2 * CONV_PAD), ls]
            acc = None
            for s in range(SUBLANES):
                xs = win if s == 0 else win[s:s + nshift * SUBLANES, :]
                xs = xs.reshape(-1, SUBLANES, LANES)
                for a in range(2 * CONV_PAD // SUBLANES):
                    j = SUBLANES * a + s - off
                    if 0 <= j < CONV_W:
                        term = xs[a:a + ngrp] * w_ref[0, j][:, ls][None]
                        acc = term if acc is None else acc + term
            o_ref[pl.ds(o0, CONV_CH), ls] = acc.reshape(CONV_CH, LANES) + bias[:, ls]
        y = o_ref[pl.ds(o0, CONV_CH), :]
        yc = y - jnp.mean(y, axis=-1, keepdims=True)
        yn = yc * lax.rsqrt(jnp.mean(yc * yc, axis=-1, keepdims=True) + EPS)
        o_ref[pl.ds(o0, CONV_CH), :] = _silu(yn * lg + lb)
        return carry

    lax.fori_loop(0, nrows * rlen // CONV_CH, body, 0)


def _conv(proj, conv_w, conv_b, ln_g, ln_b, layer, bsz, t, rlen):
    tb = max(rlen, min(t, 512))
    nrows = tb // rlen
    nt = t // tb
    vec = pl.BlockSpec((1, 1, MIX_CONV), lambda b, i: (layer, 0, 0))
    return pl.pallas_call(
        functools.partial(_conv_kernel, nrows=nrows, rlen=rlen),
        grid=(bsz, nt),
        in_specs=[
            pl.BlockSpec((tb, COLBLK), lambda b, i: (b * nt + i, 3)),
            pl.BlockSpec((tb, COLBLK), lambda b, i: (b * nt + i, 4)),
            pl.BlockSpec((1, CONV_W, SUBLANES, MIX_CONV), lambda b, i: (layer, 0, 0, 0)),
            vec, vec, vec,
        ],
        out_specs=pl.BlockSpec((tb, MIX_CONV), lambda b, i: (b * nt + i, 0)),
        out_shape=jax.ShapeDtypeStruct((bsz * t, MIX_CONV), F32),
        scratch_shapes=[pltpu.VMEM((nrows * (rlen + 2 * CONV_PAD), MIX_CONV), F32)],
        compiler_params=_cparams(),
        name="conv",
    )(proj, proj, conv_w, conv_b, ln_g, ln_b)


def _outproj_kernel(o_ref, r_ref, u_ref, x_ref, mod_ref, gn_ref, wo_ref, g2_ref, wr_ref,
                    xo_ref, h2_ref, aff_ref):
    o = o_ref[...]
    r = r_ref[...]
    gn = gn_ref[0]
    parts = []
    for h in range(GLA_HEADS):
        hs = slice(h * GLA_DV, (h + 1) * GLA_DV)
        parts.append((_rms(o[:, hs]) * gn) * _silu(r[:, hs]))
    parts.append(u_ref[...])
    a = jnp.concatenate(parts, axis=1).astype(BF16)
    m = mod_ref[0]
    gt1 = m[:, 2 * D_MODEL:3 * D_MODEL]
    sh2 = m[:, 3 * D_MODEL:4 * D_MODEL]
    sc2 = m[:, 4 * D_MODEL:5 * D_MODEL]
    x = x_ref[...] + gt1 * _dot(a, wo_ref[0])
    xo_ref[...] = x
    h2 = ((_rms(x) * g2_ref[0]) * (1.0 + sc2) + sh2).astype(BF16)
    h2_ref[...] = h2
    logits = _dot(h2, wr_ref[0])
    lane = lax.broadcasted_iota(I32, logits.shape, 1)
    logits = jnp.where(lane < N_EXPERTS, logits, -jnp.inf)
    ex = jnp.exp(logits - jnp.max(logits, axis=-1, keepdims=True))
    aff_ref[...] = ex / jnp.sum(ex, axis=-1, keepdims=True)


def _outproj(o, proj, u, x, modp, gla_norm_g, w_out_b, norm_ffn_g, w_router_b, layer, bsz, t):
    tm = min(t, 512)
    nt = t // tm
    row = lambda b, i: (b * nt + i, 0)
    return pl.pallas_call(
        _outproj_kernel,
        grid=(bsz, nt),
        in_specs=[
            pl.BlockSpec((tm, MIX_GLA), row),
            pl.BlockSpec((tm, COLBLK), lambda b, i: (b * nt + i, 2)),
            pl.BlockSpec((tm, MIX_CONV), row),
            pl.BlockSpec((tm, D_MODEL), row),
            pl.BlockSpec((1, 1, 6 * D_MODEL), lambda b, i: (b, 0, 0)),
            pl.BlockSpec((1, 1, GLA_DV), lambda b, i: (layer, 0, 0)),
            pl.BlockSpec((1, D_MODEL, D_MODEL), lambda b, i: (layer, 0, 0)),
            pl.BlockSpec((1, 1, D_MODEL), lambda b, i: (layer, 0, 0)),
            pl.BlockSpec((1, D_MODEL, LANES), lambda b, i: (layer, 0, 0)),
        ],
        out_specs=[
            pl.BlockSpec((tm, D_MODEL), row),
            pl.BlockSpec((tm, D_MODEL), row),
            pl.BlockSpec((tm, LANES), row),
        ],
        out_shape=[
            jax.ShapeDtypeStruct((bsz * t, D_MODEL), F32),
            jax.ShapeDtypeStruct((bsz * t, D_MODEL), BF16),
            jax.ShapeDtypeStruct((bsz * t, LANES), F32),
        ],
        compiler_params=_cparams(),
        name="outproj",
    )(o, proj, u, x, modp, gla_norm_g, w_out_b, norm_ffn_g, w_router_b)


PREFIX_BLK = 256


def _excl_prefix(x, t):
    blk = min(t, PREFIX_BLK)
    ri = lax.broadcasted_iota(I32, (blk, blk), 0)
    ci = lax.broadcasted_iota(I32, (blk, blk), 1)
    upper = jnp.where(ri < ci, 1.0, 0.0).astype(BF16)
    run = jnp.zeros((x.shape[0], 1), F32)
    outs = []
    for j in range(t // blk):
        xb = x[:, j * blk:(j + 1) * blk]
        outs.append(_dot(xb.astype(BF16), upper) + run)
        run = run + jnp.sum(xb, axis=1, keepdims=True)
    return jnp.concatenate(outs, axis=1) if len(outs) > 1 else outs[0]


F32_MAGNITUDE_BITS = 31
REFINE_STEPS = 16


def _route_kernel(aff_ref, sel_ref, selt_ref, afft_ref, at_s, *, bsz, t, cap):
    for b in range(bsz):
        at_s[b * N_EXPERTS:(b + 1) * N_EXPERTS, :] = aff_ref[b * t:(b + 1) * t, :].T[0:N_EXPERTS, :]
    a = at_s[...]
    afft_ref[...] = a
    rows = bsz * N_EXPERTS
    capf = float(cap)

    def enough(th):
        return jnp.sum(jnp.where(a >= th, 1.0, 0.0), axis=1, keepdims=True) >= capf

    def bit_step(i, thr):
        cand = thr | jnp.left_shift(jnp.int32(1), F32_MAGNITUDE_BITS - 1 - i)
        return jnp.where(enough(lax.bitcast_convert_type(cand, F32)), cand, thr)

    thr = lax.fori_loop(0, F32_MAGNITUDE_BITS, bit_step, jnp.zeros((rows, 1), I32))
    lo = lax.bitcast_convert_type(thr, F32)
    hi = lax.bitcast_convert_type(thr + 1, F32)

    def refine(i, c):
        lo, hi = c
        mid = lo + (hi - lo) * 0.5
        ok = enough(mid)
        return jnp.where(ok, mid, lo), jnp.where(ok, hi, mid)

    lo, hi = lax.fori_loop(0, REFINE_STEPS, refine, (lo, hi))
    gt = jnp.where(a >= hi, 1.0, 0.0) * jnp.where(a > lo, 1.0, 0.0)
    eq = jnp.where(a >= lo, 1.0, 0.0) - gt
    need = capf - jnp.sum(gt, axis=1, keepdims=True)
    tie_rank = _excl_prefix(eq, t)
    m = gt + eq * jnp.where(tie_rank < need, 1.0, 0.0)
    slot = _excl_prefix(m, t)
    sel = jnp.where(m > 0.5, slot.astype(I32), -1)
    sel_ref[...] = sel
    filler = jnp.full((LANES - N_EXPERTS, t), -1, I32)
    for b in range(bsz):
        selt_ref[b * t:(b + 1) * t, :] = jnp.concatenate(
            [sel[b * N_EXPERTS:(b + 1) * N_EXPERTS, :], filler], axis=0).T


def _route(aff, bsz, t, cap):
    rows = bsz * N_EXPERTS
    sel, selt, afft = pl.pallas_call(
        functools.partial(_route_kernel, bsz=bsz, t=t, cap=cap),
        grid=(1,),
        in_specs=[pl.BlockSpec((bsz * t, LANES), lambda i: (0, 0))],
        out_specs=[
            pl.BlockSpec((rows, t), lambda i: (0, 0)),
            pl.BlockSpec((bsz * t, LANES), lambda i: (0, 0)),
            pl.BlockSpec((rows, t), lambda i: (0, 0)),
        ],
        out_shape=[
            jax.ShapeDtypeStruct((rows, t), I32),
            jax.ShapeDtypeStruct((bsz * t, LANES), I32),
            jax.ShapeDtypeStruct((rows, t), F32),
        ],
        scratch_shapes=[pltpu.VMEM((rows, t), F32)],
        compiler_params=_cparams(),
        name="route",
    )(aff)
    return sel.reshape(bsz, N_EXPERTS, t), selt, afft.reshape(bsz, N_EXPERTS, t)


def _gather_kernel(h_ref, sel_ref, afft_ref, xs_ref, gate_ref, *, t, cap, eg):
    g = pl.program_id(1)
    hb = h_ref[...]
    slot = lax.broadcasted_iota(I32, (cap, t), 0)
    for el in range(eg):
        if eg == N_EXPERTS:
            srow = sel_ref[0, el:el + 1, :]
            arow = afft_ref[0, el:el + 1, :]
        else:
            e = g * eg + el
            srow = sel_ref[0, pl.ds(e, 1), :]
            arow = afft_ref[0, pl.ds(e, 1), :]
        hit = srow == slot
        p = jnp.where(hit, 1.0, 0.0).astype(BF16)
        xs_ref[el, 0] = _dot(p, hb).astype(BF16)
        gate_ref[el, 0] = jnp.sum(jnp.where(hit, arow, 0.0), axis=1, keepdims=True)


def _gather(h2, sel, afft, bsz, t, cap):
    eg = max(1, min(N_EXPERTS, 512 // cap))
    return pl.pallas_call(
        functools.partial(_gather_kernel, t=t, cap=cap, eg=eg),
        grid=(bsz, N_EXPERTS // eg),
        in_specs=[
            pl.BlockSpec((t, D_MODEL), lambda b, g: (b, 0)),
            pl.BlockSpec((1, N_EXPERTS, t), lambda b, g: (b, 0, 0)),
            pl.BlockSpec((1, N_EXPERTS, t), lambda b, g: (b, 0, 0)),
        ],
        out_specs=[
            pl.BlockSpec((eg, 1, cap, D_MODEL), lambda b, g: (g, b, 0, 0)),
            pl.BlockSpec((eg, 1, cap, 1), lambda b, g: (g, b, 0, 0)),
        ],
        out_shape=[
            jax.ShapeDtypeStruct((N_EXPERTS, bsz, cap, D_MODEL), BF16),
            jax.ShapeDtypeStruct((N_EXPERTS, bsz, cap, 1), F32),
        ],
        compiler_params=_cparams(),
        name="gather",
    )(h2, sel, afft)


def _ffn_kernel(x_ref, gate_ref, wg_ref, wu_ref, wd_ref, y_ref, wg_s, wu_s, wd_s):
    @pl.when(pl.program_id(1) == 0)
    def _():
        wg_s[...] = wg_ref[0, 0].astype(BF16)
        wu_s[...] = wu_ref[0, 0].astype(BF16)
        wd_s[...] = wd_ref[0, 0].astype(BF16)

    x = x_ref[0]
    a = _dot(x, wg_s[...])
    u = _dot(x, wu_s[...])
    hm = (_silu(a) * u).astype(BF16)
    y_ref[0] = (_dot(hm, wd_s[...]) * gate_ref[0]).astype(BF16)


def _ffn(xs, gates, w_e_gate, w_e_up, w_e_down, layer):
    rows = xs.shape[1]
    tm = min(rows, 512)
    wspec = pl.BlockSpec((1, 1, D_MODEL, D_EXPERT), lambda e, i: (layer, e, 0, 0))
    return pl.pallas_call(
        _ffn_kernel,
        grid=(N_EXPERTS, rows // tm),
        in_specs=[
            pl.BlockSpec((1, tm, D_MODEL), lambda e, i: (e, i, 0)),
            pl.BlockSpec((1, tm, 1), lambda e, i: (e, i, 0)),
            wspec, wspec,
            pl.BlockSpec((1, 1, D_EXPERT, D_MODEL), lambda e, i: (layer, e, 0, 0)),
        ],
        out_specs=pl.BlockSpec((1, tm, D_MODEL), lambda e, i: (e, i, 0)),
        out_shape=jax.ShapeDtypeStruct((N_EXPERTS, rows, D_MODEL), BF16),
        scratch_shapes=[
            pltpu.VMEM((D_MODEL, D_EXPERT), BF16),
            pltpu.VMEM((D_MODEL, D_EXPERT), BF16),
            pltpu.VMEM((D_EXPERT, D_MODEL), BF16),
        ],
        compiler_params=_cparams(),
        name="ffn",
    )(xs, gates, w_e_gate, w_e_up, w_e_down)


def _scatter_kernel(*refs, cap, final):
    if final:
        selt_ref, y_ref, x_ref, mod_ref, gf_ref, xo_ref, pt_s = refs
    else:
        selt_ref, y_ref, x_ref, mod_ref, xo_ref, pt_s = refs
    tt = selt_ref.shape[0]
    ncol = N_EXPERTS * cap
    gw = min(ncol, 1024)
    stf = selt_ref[...].astype(F32).astype(BF16)
    col = lax.broadcasted_iota(I32, (LANES, gw), 1)
    row = lax.broadcasted_iota(I32, (LANES, gw), 0)
    cmod = (lax.broadcasted_iota(I32, (1, gw), 1) % cap).astype(F32)
    for g in range(ncol // gw):
        expand = jnp.where((col + g * gw) // cap == row, 1.0, 0.0).astype(BF16)
        selx = _dot(stf, expand)
        pt_s[:, g * gw:(g + 1) * gw] = jnp.where(selx == cmod, 1.0, 0.0).astype(BF16)
    yb = y_ref[:, 0].reshape(ncol, D_MODEL)
    gt2 = mod_ref[0][:, 5 * D_MODEL:6 * D_MODEL]
    x = x_ref[...] + gt2 * _dot(pt_s[...], yb)
    if final:
        x = _rms(x) * gf_ref[...]
    xo_ref[...] = x


def _scatter(selt, y, x, modp, norm_final_g, bsz, t, cap):
    final = norm_final_g is not None
    tt = min(t, 512)
    nt = t // tt
    row = lambda b, i: (b * nt + i, 0)
    in_specs = [
        pl.BlockSpec((tt, LANES), row),
        pl.BlockSpec((N_EXPERTS, 1, cap, D_MODEL), lambda b, i: (0, b, 0, 0)),
        pl.BlockSpec((tt, D_MODEL), row),
        pl.BlockSpec((1, 1, 6 * D_MODEL), lambda b, i: (b, 0, 0)),
    ]
    args = [selt, y, x, modp]
    if final:
        in_specs.append(pl.BlockSpec((1, D_MODEL), lambda b, i: (0, 0)))
        args.append(norm_final_g)
    return pl.pallas_call(
        functools.partial(_scatter_kernel, cap=cap, final=final),
        grid=(bsz, nt),
        in_specs=in_specs,
        out_specs=pl.BlockSpec((tt, D_MODEL), row),
        out_shape=jax.ShapeDtypeStruct((bsz * t, D_MODEL), F32),
        scratch_shapes=[pltpu.VMEM((tt, N_EXPERTS * cap), BF16)],
        compiler_params=_cparams(),
        name="scatter",
    )(*args)


def _layer(x, modp, s0f, s0b, wts, layer, bsz, t, rlen, emit_state, final_g):
    cap = EC_FACTOR * t // N_EXPERTS
    proj = _inproj(x, modp, wts["norm_mix_g"], wts["w_in"], layer, bsz, t)
    gla = _gla(proj, wts["wdf"], wts["bdf"], wts["wdb"], wts["bdb"], s0f, s0b, layer, bsz, t, emit_state)
    if emit_state:
        o, sf, sb = gla
    else:
        (o,) = gla
        sf = sb = None
    u = _conv(proj, wts["conv_w"], wts["conv_b"], wts["conv_ln_g"], wts["conv_ln_b"], layer, bsz, t, rlen)
    x, h2, aff = _outproj(o, proj, u, x, modp, wts["gla_norm_g"], wts["w_out"], wts["norm_ffn_g"],
                          wts["w_router"], layer, bsz, t)
    sel, selt, afft = _route(aff, bsz, t, cap)
    xs, gates = _gather(h2, sel, afft, bsz, t, cap)
    y = _ffn(xs.reshape(N_EXPERTS, bsz * cap, D_MODEL), gates.reshape(N_EXPERTS, bsz * cap, 1),
             wts["w_e_gate"], wts["w_e_up"], wts["w_e_down"], layer)
    x = _scatter(selt, y.reshape(N_EXPERTS, bsz, cap, D_MODEL), x, modp, final_g, bsz, t, cap)
    return x, sf, sb


def kernel(x_prompt, x_sample, state_gla_fwd, state_gla_bwd, c, c_ctx, norm_mix_g, norm_ffn_g, norm_final_g, w_mod, b_mod, w_in, w_decay_f, b_decay_f, w_decay_b, b_decay_b, gla_norm_g, conv_w, conv_b, conv_ln_g, conv_ln_b, w_out, w_router, w_e_gate, w_e_up, w_e_down):
    depth = w_in.shape[0]
    bp, tp, _ = x_prompt.shape
    bs, ts, _ = x_sample.shape

    zcols = 2 * GLA_KEY + 2 * MIX_GLA
    w_in_r = jnp.concatenate(
        [w_in[:, :, :zcols], w_in[:, :, zcols + 2 * GATE_RANK:], w_in[:, :, zcols:zcols + 2 * GATE_RANK],
         jnp.zeros((depth, D_MODEL, LANES - 2 * GATE_RANK), F32)], axis=2).astype(BF16)
    zpad_f = jnp.zeros((depth, LANES - GATE_RANK, GLA_KEY), F32)
    zpad_b0 = jnp.zeros((depth, GATE_RANK, GLA_KEY), F32)
    zpad_b1 = jnp.zeros((depth, LANES - 2 * GATE_RANK, GLA_KEY), F32)
    vec = lambda a: a.reshape(depth, 1, a.shape[-1])
    wts = dict(
        norm_mix_g=vec(norm_mix_g), norm_ffn_g=vec(norm_ffn_g), w_in=w_in_r,
        wdf=jnp.concatenate([w_decay_f, zpad_f], axis=1).astype(BF16), bdf=vec(b_decay_f),
        wdb=jnp.concatenate([zpad_b0, w_decay_b, zpad_b1], axis=1).astype(BF16), bdb=vec(b_decay_b),
        gla_norm_g=vec(gla_norm_g),
        conv_w=jnp.broadcast_to(conv_w[:, :, None, :], (depth, CONV_W, SUBLANES, MIX_CONV)),
        conv_b=vec(conv_b),
        conv_ln_g=vec(conv_ln_g), conv_ln_b=vec(conv_ln_b), w_out=w_out.astype(BF16),
        w_router=jnp.concatenate(
            [w_router, jnp.zeros((depth, D_MODEL, LANES - N_EXPERTS), F32)], axis=2).astype(BF16),
        w_e_gate=w_e_gate, w_e_up=w_e_up, w_e_down=w_e_down,
    )

    nrow = 2 * SUBLANES
    cond = jnp.concatenate([c, c_ctx[None, :], jnp.zeros((nrow - bs - 1, D_MODEL), F32)], axis=0)
    mod = _modulation(cond, w_mod, b_mod)

    xp = x_prompt.reshape(bp * tp, D_MODEL)
    xs = x_sample.reshape(bs * ts, D_MODEL)
    final_g = norm_final_g.reshape(1, D_MODEL)
    fwd_states = []
    bwd_states = []
    for l in range(depth):
        last = l == depth - 1
        mod_ctx = jnp.broadcast_to(mod[l, bs][None, None, :], (bp, 1, 6 * D_MODEL))
        mod_lat = mod[l, :bs][:, None, :]
        xp, sf, sb = _layer(xp, mod_ctx, None, None, wts, l, bp, tp, tp, True, final_g if last else None)
        fwd_states.append(sf)
        bwd_states.append(sb)
        xs, _, _ = _layer(xs, mod_lat, state_gla_fwd, state_gla_bwd, wts, l, bs, ts, GRID_W, False,
                          final_g if last else None)
    y_prompt = xp.reshape(bp, tp, D_MODEL)
    y_sample = xs.reshape(bs, ts, D_MODEL)
    return (y_prompt, y_sample, jnp.concatenate(fwd_states, axis=1), jnp.concatenate(bwd_states, axis=1))
```

```python
import functools

import jax
import jax.numpy as jnp
from jax import lax
from jax.experimental import pallas as pl
from jax.experimental.pallas import tpu as pltpu

F32 = jnp.float32
BF16 = jnp.bfloat16
I32 = jnp.int32

D_MODEL = 1024
GRID_W = 64
MIX_GLA = 512
MIX_CONV = 512
GLA_HEADS = 4
GLA_DK = 64
GLA_DV = 128
GLA_KEY = GLA_HEADS * GLA_DK
GATE_RANK = 16
GATE_TAU = 16.0
CHUNK = 64
CONV_W = 31
CONV_HALF = CONV_W // 2
N_EXPERTS = 16
EC_FACTOR = 2
D_EXPERT = 1024
EPS = 1e-6

LANES = 128
SUBLANES = 8

PROJ_COLS = 2 * GLA_KEY + 2 * MIX_GLA + 2 * MIX_CONV + LANES
COLBLK = 512
Z_COLBLK = (2 * GLA_KEY + 2 * MIX_GLA + 2 * MIX_CONV) // LANES

VMEM_LIMIT = 56 * 1024 * 1024
CONV_PAD = 16
CONV_CH = 64
GLA_GROUP = 4


def _cparams():
    return pltpu.CompilerParams(vmem_limit_bytes=VMEM_LIMIT)


def _silu(x):
    return x * jax.nn.sigmoid(x)


def _rms(x):
    return x * lax.rsqrt(jnp.mean(x * x, axis=-1, keepdims=True) + EPS)


def _dot(a, b):
    return jnp.dot(a, b, preferred_element_type=F32)


def _mod_kernel(c_ref, w_ref, b_ref, o_ref):
    s = _silu(c_ref[...]).astype(BF16)
    o_ref[0] = _dot(s, w_ref[0].astype(BF16)) + b_ref[0]


def _modulation(cond, w_mod, b_mod):
    depth = w_mod.shape[0]
    n_out = w_mod.shape[2]
    tn = 1536
    rows = cond.shape[0]
    return pl.pallas_call(
        _mod_kernel,
        grid=(depth, n_out // tn),
        in_specs=[
            pl.BlockSpec((rows, D_MODEL), lambda l, j: (0, 0)),
            pl.BlockSpec((1, D_MODEL, tn), lambda l, j: (l, 0, j)),
            pl.BlockSpec((1, 1, tn), lambda l, j: (l, 0, j)),
        ],
        out_specs=pl.BlockSpec((1, rows, tn), lambda l, j: (l, 0, j)),
        out_shape=jax.ShapeDtypeStruct((depth, rows, n_out), F32),
        compiler_params=_cparams(),
        name="modulation",
    )(cond, w_mod, b_mod.reshape(depth, 1, n_out))


def _inproj_kernel(x_ref, mod_ref, g_ref, w_ref, o_ref):
    m = mod_ref[0]
    sh = m[:, 0:D_MODEL]
    sc = m[:, D_MODEL:2 * D_MODEL]
    h = (_rms(x_ref[...]) * g_ref[0]) * (1.0 + sc) + sh
    o_ref[...] = _dot(h.astype(BF16), w_ref[0])


def _inproj(x, modp, norm_g, w_in_b, layer, bsz, t):
    tm = min(t, 512)
    nt = t // tm
    return pl.pallas_call(
        _inproj_kernel,
        grid=(bsz, nt),
        in_specs=[
            pl.BlockSpec((tm, D_MODEL), lambda b, i: (b * nt + i, 0)),
            pl.BlockSpec((1, 1, 6 * D_MODEL), lambda b, i: (b, 0, 0)),
            pl.BlockSpec((1, 1, D_MODEL), lambda b, i: (layer, 0, 0)),
            pl.BlockSpec((1, D_MODEL, PROJ_COLS), lambda b, i: (layer, 0, 0)),
        ],
        out_specs=pl.BlockSpec((tm, PROJ_COLS), lambda b, i: (b * nt + i, 0)),
        out_shape=jax.ShapeDtypeStruct((bsz * t, PROJ_COLS), F32),
        compiler_params=_cparams(),
        name="inproj",
    )(x, modp, norm_g, w_in_b)


def _log_sigmoid(x):
    return jnp.minimum(x, 0.0) - jnp.log(1.0 + jnp.exp(-jnp.abs(x)))


def _head_stack(x, lane_head):
    return jnp.concatenate([jnp.where(lane_head == h, x, 0.0) for h in range(GLA_HEADS)], axis=0).astype(BF16)


def _gla_group(chains, lane_head, s_ref, rhs_ref):
    cums = []
    for qk, v, g, tri, causal, d, last_row in chains:
        g_hi = g.astype(BF16)
        g_lo = (g - g_hi.astype(F32)).astype(BF16)
        cums.append(_dot(tri, g_hi) + _dot(tri, g_lo))
    pre = []
    for (qk, v, g, tri, causal, d, last_row), cum in zip(chains, cums):
        cl = cum[last_row:last_row + 1, :]
        q = qk[:, 0:GLA_KEY]
        k = qk[:, GLA_KEY:2 * GLA_KEY]
        qd = (q * (jnp.exp(cum) * (GLA_DK ** -0.5))).astype(BF16)
        k_inv = k * jnp.exp(-cum)
        k_end = k_inv * jnp.exp(cl)
        tr = jnp.concatenate([k_end, cum], axis=0).T
        pre.append((qd, _head_stack(k_inv, lane_head), tr, v.astype(BF16)))
    mm = []
    for (qk, v, g, tri, causal, d, last_row), (qd, ks, tr, vb) in zip(chains, pre):
        sc = lax.dot_general(qd, ks, (((1,), (1,)), ((), ())), preferred_element_type=F32)
        upd = [_dot(tr[h * GLA_DK:(h + 1) * GLA_DK, 0:CHUNK].astype(BF16),
                    vb[:, h * GLA_DV:(h + 1) * GLA_DV]) for h in range(GLA_HEADS)]
        mm.append((sc, jnp.concatenate(upd, axis=0)))
    state = {}
    lhs = []
    for i, ((qk, v, g, tri, causal, d, last_row), (qd, ks, tr, vb), (sc, upd)) in enumerate(zip(chains, pre, mm)):
        s_prev = state[d] if d in state else s_ref[d]
        s_prev_b = s_prev.astype(BF16)
        for h in range(GLA_HEADS):
            rows = slice(h * GLA_DK, (h + 1) * GLA_DK)
            cols = slice(h * GLA_DV, (h + 1) * GLA_DV)
            rhs_ref[i, rows, cols] = vb[:, cols]
            rhs_ref[i, GLA_KEY + h * GLA_DK:GLA_KEY + (h + 1) * GLA_DK, cols] = s_prev_b[rows, :]
        decay = jnp.exp(tr[:, CHUNK + last_row:CHUNK + last_row + 1])
        state[d] = decay * s_prev + upd
        lhs.append(jnp.concatenate([(sc * causal).astype(BF16), qd], axis=1))
    outs = [_dot(l, rhs_ref[i]) for i, l in enumerate(lhs)]
    for d, s_new in state.items():
        s_ref[d] = s_new
    return outs


def _gla_kernel(*refs, t, zero_init, emit_state):
    qk_ref, v_ref, z_ref, wf_ref, bf_ref, wb_ref, bb_ref = refs[:7]
    pos = 7
    if not zero_init:
        s0f_ref, s0b_ref = refs[pos:pos + 2]
        pos += 2
    o_ref = refs[pos]
    pos += 1
    if emit_state:
        sf_ref, sb_ref = refs[pos:pos + 2]
        pos += 2
    gf_s, gb_s, s_ref, mask_s, rhs_s = refs[pos:pos + 5]
    rhs_s[...] = jnp.zeros(rhs_s.shape, BF16)

    z = z_ref[...].astype(BF16)
    gf_s[...] = _log_sigmoid(_dot(z, wf_ref[0]) + bf_ref[0]) * (1.0 / GATE_TAU)
    gb_s[...] = _log_sigmoid(_dot(z, wb_ref[0]) + bb_ref[0]) * (1.0 / GATE_TAU)
    if zero_init:
        s_ref[...] = jnp.zeros(s_ref.shape, F32)
    else:
        s_ref[0] = s0f_ref[0, 0].reshape(GLA_KEY, GLA_DV)
        s_ref[1] = s0b_ref[0, 0].reshape(GLA_KEY, GLA_DV)

    ri = lax.broadcasted_iota(I32, (CHUNK, CHUNK), 0)
    ci = lax.broadcasted_iota(I32, (CHUNK, CHUNK), 1)
    tri_low = jnp.where(ri >= ci, 1.0, 0.0).astype(BF16)
    tri_upp = jnp.where(ri <= ci, 1.0, 0.0).astype(BF16)
    rs = lax.broadcasted_iota(I32, (CHUNK, GLA_HEADS * CHUNK), 0)
    cs = lax.broadcasted_iota(I32, (CHUNK, GLA_HEADS * CHUNK), 1) % CHUNK
    mask_s[0] = jnp.where(rs >= cs, 1.0, 0.0)
    mask_s[1] = jnp.where(rs <= cs, 1.0, 0.0)
    lane_head = lax.broadcasted_iota(I32, (CHUNK, GLA_KEY), 1) // GLA_DK
    n = t // CHUNK

    group = min(GLA_GROUP, n // 2)

    def step(i, first):
        rows = []
        chains = []
        for j in range(group):
            c = i * group + j
            rf = pl.multiple_of(c * CHUNK, CHUNK)
            rb = pl.multiple_of((n - 1 - c) * CHUNK, CHUNK)
            rows += [rf, rb]
            chains.append((qk_ref[pl.ds(rf, CHUNK), :], v_ref[pl.ds(rf, CHUNK), :],
                           gf_s[pl.ds(rf, CHUNK), :], tri_low, mask_s[0], 0, CHUNK - 1))
            chains.append((qk_ref[pl.ds(rb, CHUNK), :], v_ref[pl.ds(rb, CHUNK), :],
                           gb_s[pl.ds(rb, CHUNK), :], tri_upp, mask_s[1], 1, 0))
        outs = _gla_group(chains, lane_head, s_ref, rhs_s)
        for r, o in zip(rows, outs):
            if first:
                o_ref[pl.ds(r, CHUNK), :] = o
            else:
                o_ref[pl.ds(r, CHUNK), :] += o

    def first_half(i, c):
        step(i, True)
        return c

    def second_half(i, c):
        step(i, False)
        return c

    half = n // (2 * group)
    lax.fori_loop(0, half, first_half, 0)
    lax.fori_loop(half, 2 * half, second_half, 0)
    if emit_state:
        sf_ref[0, 0] = s_ref[0].reshape(GLA_HEADS, GLA_DK, GLA_DV)
        sb_ref[0, 0] = s_ref[1].reshape(GLA_HEADS, GLA_DK, GLA_DV)


def _gla(proj, wdf, bdf, wdb, bdb, s0f, s0b, layer, bsz, t, emit_state):
    zero_init = s0f is None
    st_block = (1, 1, GLA_HEADS, GLA_DK, GLA_DV)
    in_specs = [
        pl.BlockSpec((t, COLBLK), lambda b: (b, 0)),
        pl.BlockSpec((t, COLBLK), lambda b: (b, 1)),
        pl.BlockSpec((t, LANES), lambda b: (b, Z_COLBLK)),
        pl.BlockSpec((1, LANES, GLA_KEY), lambda b: (layer, 0, 0)),
        pl.BlockSpec((1, 1, GLA_KEY), lambda b: (layer, 0, 0)),
        pl.BlockSpec((1, LANES, GLA_KEY), lambda b: (layer, 0, 0)),
        pl.BlockSpec((1, 1, GLA_KEY), lambda b: (layer, 0, 0)),
    ]
    args = [proj, proj, proj, wdf, bdf, wdb, bdb]
    if not zero_init:
        in_specs += [pl.BlockSpec(st_block, lambda b: (b, layer, 0, 0, 0))] * 2
        args += [s0f, s0b]
    out_specs = [pl.BlockSpec((t, MIX_GLA), lambda b: (b, 0))]
    out_shape = [jax.ShapeDtypeStruct((bsz * t, MIX_GLA), F32)]
    if emit_state:
        out_specs += [pl.BlockSpec((1, 1, GLA_HEADS, GLA_DK, GLA_DV), lambda b: (b, 0, 0, 0, 0))] * 2
        out_shape += [jax.ShapeDtypeStruct((bsz, 1, GLA_HEADS, GLA_DK, GLA_DV), F32)] * 2
    return pl.pallas_call(
        functools.partial(_gla_kernel, t=t, zero_init=zero_init, emit_state=emit_state),
        grid=(bsz,),
        in_specs=in_specs,
        out_specs=out_specs,
        out_shape=out_shape,
        scratch_shapes=[
            pltpu.VMEM((t, GLA_KEY), F32),
            pltpu.VMEM((t, GLA_KEY), F32),
            pltpu.VMEM((2, GLA_KEY, GLA_DV), F32),
            pltpu.VMEM((2, CHUNK, GLA_HEADS * CHUNK), F32),
            pltpu.VMEM((2 * min(GLA_GROUP, t // CHUNK // 2), 2 * GLA_KEY, MIX_GLA), BF16),
        ],
        compiler_params=_cparams(),
        name="gla",
    )(*args)


def _conv_kernel(ga_ref, gg_ref, w_ref, b_ref, lg_ref, lb_ref, o_ref, pad_ref, y_ref, *, nrows, rlen):
    stride = rlen + 2 * CONV_PAD
    ntile = MIX_CONV // LANES
    u = ga_ref[...] * jax.nn.sigmoid(gg_ref[...])
    zeros = jnp.zeros((CONV_PAD, LANES), F32)
    for lt in range(ntile):
        for r in range(nrows):
            base = r * stride
            pad_ref[lt, base:base + CONV_PAD, :] = zeros
            pad_ref[lt, base + CONV_PAD:base + CONV_PAD + rlen, :] = u[r * rlen:(r + 1) * rlen,
                                                                       lt * LANES:(lt + 1) * LANES]
            pad_ref[lt, base + CONV_PAD + rlen:base + stride, :] = zeros
    lg = lg_ref[0]
    lb = lb_ref[0]
    assert rlen == CONV_CH or nrows == 1
    pstep = stride if rlen == CONV_CH else CONV_CH
    off = CONV_PAD - CONV_HALF
    ngrp = CONV_CH // SUBLANES
    nshift = (CONV_CH + 2 * CONV_PAD) // SUBLANES - 1

    def chunk(c, carry):
        p0 = pl.multiple_of(c * pstep, SUBLANES)
        g0 = c * ngrp

        def lane_tile(lt, carry2):
            win = pad_ref[lt, pl.ds(p0, CONV_CH + 2 * CONV_PAD), :]
            acc = None
            for s in range(SUBLANES):
                xs = win if s == 0 else win[s:s + nshift * SUBLANES, :]
                xs = xs.reshape(-1, SUBLANES, LANES)
                for a in range(2 * CONV_PAD // SUBLANES):
                    j = SUBLANES * a + s - off
                    if 0 <= j < CONV_W:
                        term = xs[a:a + ngrp] * w_ref[0, lt, j][None]
                        acc = term if acc is None else acc + term
            y_ref[lt, pl.ds(g0, ngrp)] = acc + b_ref[0, lt][None]
            return carry2

        lax.fori_loop(0, ntile, lane_tile, 0)
        return carry

    lax.fori_loop(0, nrows * rlen // CONV_CH, chunk, 0)
    y = jnp.concatenate([y_ref[lt].reshape(nrows * rlen, LANES) for lt in range(ntile)], axis=1)
    yc = y - jnp.mean(y, axis=-1, keepdims=True)
    yn = yc * lax.rsqrt(jnp.mean(yc * yc, axis=-1, keepdims=True) + EPS)
    o_ref[...] = _silu(yn * lg + lb)


def _conv(proj, conv_w, conv_b, ln_g, ln_b, layer, bsz, t, rlen):
    tb = max(rlen, min(t, 512))
    nrows = tb // rlen
    nt = t // tb
    ntile = MIX_CONV // LANES
    vec = pl.BlockSpec((1, 1, MIX_CONV), lambda b, i: (layer, 0, 0))
    return pl.pallas_call(
        functools.partial(_conv_kernel, nrows=nrows, rlen=rlen),
        grid=(bsz, nt),
        in_specs=[
            pl.BlockSpec((tb, COLBLK), lambda b, i: (b * nt + i, 3)),
            pl.BlockSpec((tb, COLBLK), lambda b, i: (b * nt + i, 4)),
            pl.BlockSpec((1, ntile, CONV_W, SUBLANES, LANES), lambda b, i: (layer, 0, 0, 0, 0)),
            pl.BlockSpec((1, ntile, SUBLANES, LANES), lambda b, i: (layer, 0, 0, 0)),
            vec, vec,
        ],
        out_specs=pl.BlockSpec((tb, MIX_CONV), lambda b, i: (b * nt + i, 0)),
        out_shape=jax.ShapeDtypeStruct((bsz * t, MIX_CONV), F32),
        scratch_shapes=[
            pltpu.VMEM((ntile, nrows * (rlen + 2 * CONV_PAD), LANES), F32),
            pltpu.VMEM((ntile, tb // SUBLANES, SUBLANES, LANES), F32),
        ],
        compiler_params=_cparams(),
        name="conv",
    )(proj, proj, conv_w, conv_b, ln_g, ln_b)


def _outproj_kernel(o_ref, r_ref, u_ref, x_ref, mod_ref, gn_ref, wo_ref, g2_ref, wr_ref,
                    xo_ref, h2_ref, aff_ref):
    o = o_ref[...]
    r = r_ref[...]
    gn = gn_ref[0]
    parts = []
    for h in range(GLA_HEADS):
        hs = slice(h * GLA_DV, (h + 1) * GLA_DV)
        parts.append((_rms(o[:, hs]) * gn) * _silu(r[:, hs]))
    parts.append(u_ref[...])
    a = jnp.concatenate(parts, axis=1).astype(BF16)
    m = mod_ref[0]
    gt1 = m[:, 2 * D_MODEL:3 * D_MODEL]
    sh2 = m[:, 3 * D_MODEL:4 * D_MODEL]
    sc2 = m[:, 4 * D_MODEL:5 * D_MODEL]
    x = x_ref[...] + gt1 * _dot(a, wo_ref[0])
    xo_ref[...] = x
    h2 = ((_rms(x) * g2_ref[0]) * (1.0 + sc2) + sh2).astype(BF16)
    h2_ref[...] = h2
    logits = _dot(h2, wr_ref[0])
    lane = lax.broadcasted_iota(I32, logits.shape, 1)
    logits = jnp.where(lane < N_EXPERTS, logits, -jnp.inf)
    ex = jnp.exp(logits - jnp.max(logits, axis=-1, keepdims=True))
    aff_ref[...] = ex / jnp.sum(ex, axis=-1, keepdims=True)


def _outproj(o, proj, u, x, modp, gla_norm_g, w_out_b, norm_ffn_g, w_router_b, layer, bsz, t):
    tm = min(t, 512)
    nt = t // tm
    row = lambda b, i: (b * nt + i, 0)
    return pl.pallas_call(
        _outproj_kernel,
        grid=(bsz, nt),
        in_specs=[
            pl.BlockSpec((tm, MIX_GLA), row),
            pl.BlockSpec((tm, COLBLK), lambda b, i: (b * nt + i, 2)),
            pl.BlockSpec((tm, MIX_CONV), row),
            pl.BlockSpec((tm, D_MODEL), row),
            pl.BlockSpec((1, 1, 6 * D_MODEL), lambda b, i: (b, 0, 0)),
            pl.BlockSpec((1, 1, GLA_DV), lambda b, i: (layer, 0, 0)),
            pl.BlockSpec((1, D_MODEL, D_MODEL), lambda b, i: (layer, 0, 0)),
            pl.BlockSpec((1, 1, D_MODEL), lambda b, i: (layer, 0, 0)),
            pl.BlockSpec((1, D_MODEL, LANES), lambda b, i: (layer, 0, 0)),
        ],
        out_specs=[
            pl.BlockSpec((tm, D_MODEL), row),
            pl.BlockSpec((tm, D_MODEL), row),
            pl.BlockSpec((tm, LANES), row),
        ],
        out_shape=[
            jax.ShapeDtypeStruct((bsz * t, D_MODEL), F32),
            jax.ShapeDtypeStruct((bsz * t, D_MODEL), BF16),
            jax.ShapeDtypeStruct((bsz * t, LANES), F32),
        ],
        compiler_params=_cparams(),
        name="outproj",
    )(o, proj, u, x, modp, gla_norm_g, w_out_b, norm_ffn_g, w_router_b)


PREFIX_BLK = 256


def _excl_prefix(x, t):
    blk = min(t, PREFIX_BLK)
    ri = lax.broadcasted_iota(I32, (blk, blk), 0)
    ci = lax.broadcasted_iota(I32, (blk, blk), 1)
    upper = jnp.where(ri < ci, 1.0, 0.0).astype(BF16)
    run = jnp.zeros((x.shape[0], 1), F32)
    outs = []
    for j in range(t // blk):
        xb = x[:, j * blk:(j + 1) * blk]
        outs.append(_dot(xb.astype(BF16), upper) + run)
        run = run + jnp.sum(xb, axis=1, keepdims=True)
    return jnp.concatenate(outs, axis=1) if len(outs) > 1 else outs[0]


F32_MAGNITUDE_BITS = 31
REFINE_STEPS = 16


def _route_kernel(aff_ref, sel_ref, selt_ref, afft_ref, at_s, *, bsz, t, cap):
    for b in range(bsz):
        at_s[b * N_EXPERTS:(b + 1) * N_EXPERTS, :] = aff_ref[b * t:(b + 1) * t, :].T[0:N_EXPERTS, :]
    a = at_s[...]
    afft_ref[...] = a
    rows = bsz * N_EXPERTS
    capf = float(cap)

    def enough(th):
        return jnp.sum(jnp.where(a >= th, 1.0, 0.0), axis=1, keepdims=True) >= capf

    def bit_step(i, thr):
        cand = thr | jnp.left_shift(jnp.int32(1), F32_MAGNITUDE_BITS - 1 - i)
        return jnp.where(enough(lax.bitcast_convert_type(cand, F32)), cand, thr)

    thr = lax.fori_loop(0, F32_MAGNITUDE_BITS, bit_step, jnp.zeros((rows, 1), I32))
    lo = lax.bitcast_convert_type(thr, F32)
    hi = lax.bitcast_convert_type(thr + 1, F32)

    def refine(i, c):
        lo, hi = c
        mid = lo + (hi - lo) * 0.5
        ok = enough(mid)
        return jnp.where(ok, mid, lo), jnp.where(ok, hi, mid)

    lo, hi = lax.fori_loop(0, REFINE_STEPS, refine, (lo, hi))
    gt = jnp.where(a >= hi, 1.0, 0.0) * jnp.where(a > lo, 1.0, 0.0)
    eq = jnp.where(a >= lo, 1.0, 0.0) - gt
    need = capf - jnp.sum(gt, axis=1, keepdims=True)
    tie_rank = _excl_prefix(eq, t)
    m = gt + eq * jnp.where(tie_rank < need, 1.0, 0.0)
    slot = _excl_prefix(m, t)
    sel = jnp.where(m > 0.5, slot.astype(I32), -1)
    sel_ref[...] = sel
    filler = jnp.full((LANES - N_EXPERTS, t), -1, I32)
    for b in range(bsz):
        selt_ref[b * t:(b + 1) * t, :] = jnp.concatenate(
            [sel[b * N_EXPERTS:(b + 1) * N_EXPERTS, :], filler], axis=0).T


def _route(aff, bsz, t, cap):
    rows = bsz * N_EXPERTS
    sel, selt, afft = pl.pallas_call(
        functools.partial(_route_kernel, bsz=bsz, t=t, cap=cap),
        grid=(1,),
        in_specs=[pl.BlockSpec((bsz * t, LANES), lambda i: (0, 0))],
        out_specs=[
            pl.BlockSpec((rows, t), lambda i: (0, 0)),
            pl.BlockSpec((bsz * t, LANES), lambda i: (0, 0)),
            pl.BlockSpec((rows, t), lambda i: (0, 0)),
        ],
        out_shape=[
            jax.ShapeDtypeStruct((rows, t), I32),
            jax.ShapeDtypeStruct((bsz * t, LANES), I32),
            jax.ShapeDtypeStruct((rows, t), F32),
        ],
        scratch_shapes=[pltpu.VMEM((rows, t), F32)],
        compiler_params=_cparams(),
        name="route",
    )(aff)
    return sel.reshape(bsz, N_EXPERTS, t), selt, afft.reshape(bsz, N_EXPERTS, t)


def _gather_kernel(h_ref, sel_ref, afft_ref, xs_ref, gate_ref, *, t, cap, eg):
    g = pl.program_id(1)
    hb = h_ref[...]
    slot = lax.broadcasted_iota(I32, (cap, t), 0)
    for el in range(eg):
        if eg == N_EXPERTS:
            srow = sel_ref[0, el:el + 1, :]
            arow = afft_ref[0, el:el + 1, :]
        else:
            e = g * eg + el
            srow = sel_ref[0, pl.ds(e, 1), :]
            arow = afft_ref[0, pl.ds(e, 1), :]
        hit = srow == slot
        p = jnp.where(hit, 1.0, 0.0).astype(BF16)
        xs_ref[el, 0] = _dot(p, hb).astype(BF16)
        gate_ref[el, 0] = jnp.sum(jnp.where(hit, arow, 0.0), axis=1, keepdims=True)


def _gather(h2, sel, afft, bsz, t, cap):
    eg = max(1, min(N_EXPERTS, 512 // cap))
    return pl.pallas_call(
        functools.partial(_gather_kernel, t=t, cap=cap, eg=eg),
        grid=(bsz, N_EXPERTS // eg),
        in_specs=[
            pl.BlockSpec((t, D_MODEL), lambda b, g: (b, 0)),
            pl.BlockSpec((1, N_EXPERTS, t), lambda b, g: (b, 0, 0)),
            pl.BlockSpec((1, N_EXPERTS, t), lambda b, g: (b, 0, 0)),
        ],
        out_specs=[
            pl.BlockSpec((eg, 1, cap, D_MODEL), lambda b, g: (g, b, 0, 0)),
            pl.BlockSpec((eg, 1, cap, 1), lambda b, g: (g, b, 0, 0)),
        ],
        out_shape=[
            jax.ShapeDtypeStruct((N_EXPERTS, bsz, cap, D_MODEL), BF16),
            jax.ShapeDtypeStruct((N_EXPERTS, bsz, cap, 1), F32),
        ],
        compiler_params=_cparams(),
        name="gather",
    )(h2, sel, afft)


def _ffn_kernel(x_ref, gate_ref, wg_ref, wu_ref, wd_ref, y_ref, wg_s, wu_s, wd_s):
    @pl.when(pl.program_id(1) == 0)
    def _():
        wg_s[...] = wg_ref[0, 0].astype(BF16)
        wu_s[...] = wu_ref[0, 0].astype(BF16)
        wd_s[...] = wd_ref[0, 0].astype(BF16)

    x = x_ref[0]
    a = _dot(x, wg_s[...])
    u = _dot(x, wu_s[...])
    hm = (_silu(a) * u).astype(BF16)
    y_ref[0] = (_dot(hm, wd_s[...]) * gate_ref[0]).astype(BF16)


def _ffn(xs, gates, w_e_gate, w_e_up, w_e_down, layer):
    rows = xs.shape[1]
    tm = min(rows, 512)
    wspec = pl.BlockSpec((1, 1, D_MODEL, D_EXPERT), lambda e, i: (layer, e, 0, 0))
    return pl.pallas_call(
        _ffn_kernel,
        grid=(N_EXPERTS, rows // tm),
        in_specs=[
            pl.BlockSpec((1, tm, D_MODEL), lambda e, i: (e, i, 0)),
            pl.BlockSpec((1, tm, 1), lambda e, i: (e, i, 0)),
            wspec, wspec,
            pl.BlockSpec((1, 1, D_EXPERT, D_MODEL), lambda e, i: (layer, e, 0, 0)),
        ],
        out_specs=pl.BlockSpec((1, tm, D_MODEL), lambda e, i: (e, i, 0)),
        out_shape=jax.ShapeDtypeStruct((N_EXPERTS, rows, D_MODEL), BF16),
        scratch_shapes=[
            pltpu.VMEM((D_MODEL, D_EXPERT), BF16),
            pltpu.VMEM((D_MODEL, D_EXPERT), BF16),
            pltpu.VMEM((D_EXPERT, D_MODEL), BF16),
        ],
        compiler_params=_cparams(),
        name="ffn",
    )(xs, gates, w_e_gate, w_e_up, w_e_down)


def _scatter_kernel(*refs, cap, final):
    if final:
        selt_ref, y_ref, x_ref, mod_ref, gf_ref, xo_ref, pt_s = refs
    else:
        selt_ref, y_ref, x_ref, mod_ref, xo_ref, pt_s = refs
    tt = selt_ref.shape[0]
    ncol = N_EXPERTS * cap
    gw = min(ncol, 1024)
    stf = selt_ref[...].astype(F32).astype(BF16)
    col = lax.broadcasted_iota(I32, (LANES, gw), 1)
    row = lax.broadcasted_iota(I32, (LANES, gw), 0)
    cmod = (lax.broadcasted_iota(I32, (1, gw), 1) % cap).astype(F32)
    for g in range(ncol // gw):
        expand = jnp.where((col + g * gw) // cap == row, 1.0, 0.0).astype(BF16)
        selx = _dot(stf, expand)
        pt_s[:, g * gw:(g + 1) * gw] = jnp.where(selx == cmod, 1.0, 0.0).astype(BF16)
    yb = y_ref[:, 0].reshape(ncol, D_MODEL)
    gt2 = mod_ref[0][:, 5 * D_MODEL:6 * D_MODEL]
    x = x_ref[...] + gt2 * _dot(pt_s[...], yb)
    if final:
        x = _rms(x) * gf_ref[...]
    xo_ref[...] = x


def _scatter(selt, y, x, modp, norm_final_g, bsz, t, cap):
    final = norm_final_g is not None
    tt = min(t, 512)
    nt = t // tt
    row = lambda b, i: (b * nt + i, 0)
    in_specs = [
        pl.BlockSpec((tt, LANES), row),
        pl.BlockSpec((N_EXPERTS, 1, cap, D_MODEL), lambda b, i: (0, b, 0, 0)),
        pl.BlockSpec((tt, D_MODEL), row),
        pl.BlockSpec((1, 1, 6 * D_MODEL), lambda b, i: (b, 0, 0)),
    ]
    args = [selt, y, x, modp]
    if final:
        in_specs.append(pl.BlockSpec((1, D_MODEL), lambda b, i: (0, 0)))
        args.append(norm_final_g)
    return pl.pallas_call(
        functools.partial(_scatter_kernel, cap=cap, final=final),
        grid=(bsz, nt),
        in_specs=in_specs,
        out_specs=pl.BlockSpec((tt, D_MODEL), row),
        out_shape=jax.ShapeDtypeStruct((bsz * t, D_MODEL), F32),
        scratch_shapes=[pltpu.VMEM((tt, N_EXPERTS * cap), BF16)],
        compiler_params=_cparams(),
        name="scatter",
    )(*args)


def _layer(x, modp, s0f, s0b, wts, layer, bsz, t, rlen, emit_state, final_g):
    cap = EC_FACTOR * t // N_EXPERTS
    proj = _inproj(x, modp, wts["norm_mix_g"], wts["w_in"], layer, bsz, t)
    gla = _gla(proj, wts["wdf"], wts["bdf"], wts["wdb"], wts["bdb"], s0f, s0b, layer, bsz, t, emit_state)
    if emit_state:
        o, sf, sb = gla
    else:
        (o,) = gla
        sf = sb = None
    u = _conv(proj, wts["conv_w"], wts["conv_b"], wts["conv_ln_g"], wts["conv_ln_b"], layer, bsz, t, rlen)
    x, h2, aff = _outproj(o, proj, u, x, modp, wts["gla_norm_g"], wts["w_out"], wts["norm_ffn_g"],
                          wts["w_router"], layer, bsz, t)
    sel, selt, afft = _route(aff, bsz, t, cap)
    xs, gates = _gather(h2, sel, afft, bsz, t, cap)
    y = _ffn(xs.reshape(N_EXPERTS, bsz * cap, D_MODEL), gates.reshape(N_EXPERTS, bsz * cap, 1),
             wts["w_e_gate"], wts["w_e_up"], wts["w_e_down"], layer)
    x = _scatter(selt, y.reshape(N_EXPERTS, bsz, cap, D_MODEL), x, modp, final_g, bsz, t, cap)
    return x, sf, sb


def kernel(x_prompt, x_sample, state_gla_fwd, state_gla_bwd, c, c_ctx, norm_mix_g, norm_ffn_g, norm_final_g, w_mod, b_mod, w_in, w_decay_f, b_decay_f, w_decay_b, b_decay_b, gla_norm_g, conv_w, conv_b, conv_ln_g, conv_ln_b, w_out, w_router, w_e_gate, w_e_up, w_e_down):
    depth = w_in.shape[0]
    bp, tp, _ = x_prompt.shape
    bs, ts, _ = x_sample.shape

    zcols = 2 * GLA_KEY + 2 * MIX_GLA
    w_in_r = jnp.concatenate(
        [w_in[:, :, :zcols], w_in[:, :, zcols + 2 * GATE_RANK:], w_in[:, :, zcols:zcols + 2 * GATE_RANK],
         jnp.zeros((depth, D_MODEL, LANES - 2 * GATE_RANK), F32)], axis=2).astype(BF16)
    zpad_f = jnp.zeros((depth, LANES - GATE_RANK, GLA_KEY), F32)
    zpad_b0 = jnp.zeros((depth, GATE_RANK, GLA_KEY), F32)
    zpad_b1 = jnp.zeros((depth, LANES - 2 * GATE_RANK, GLA_KEY), F32)
    vec = lambda a: a.reshape(depth, 1, a.shape[-1])
    wts = dict(
        norm_mix_g=vec(norm_mix_g), norm_ffn_g=vec(norm_ffn_g), w_in=w_in_r,
        wdf=jnp.concatenate([w_decay_f, zpad_f], axis=1).astype(BF16), bdf=vec(b_decay_f),
        wdb=jnp.concatenate([zpad_b0, w_decay_b, zpad_b1], axis=1).astype(BF16), bdb=vec(b_decay_b),
        gla_norm_g=vec(gla_norm_g),
        conv_w=jnp.broadcast_to(
            conv_w.reshape(depth, CONV_W, 1, MIX_CONV // LANES, LANES).transpose(0, 3, 1, 2, 4),
            (depth, MIX_CONV // LANES, CONV_W, SUBLANES, LANES)),
        conv_b=jnp.broadcast_to(conv_b.reshape(depth, MIX_CONV // LANES, 1, LANES),
                                (depth, MIX_CONV // LANES, SUBLANES, LANES)),
        conv_ln_g=vec(conv_ln_g), conv_ln_b=vec(conv_ln_b), w_out=w_out.astype(BF16),
        w_router=jnp.concatenate(
            [w_router, jnp.zeros((depth, D_MODEL, LANES - N_EXPERTS), F32)], axis=2).astype(BF16),
        w_e_gate=w_e_gate, w_e_up=w_e_up, w_e_down=w_e_down,
    )

    nrow = 2 * SUBLANES
    cond = jnp.concatenate([c, c_ctx[None, :], jnp.zeros((nrow - bs - 1, D_MODEL), F32)], axis=0)
    mod = _modulation(cond, w_mod, b_mod)

    xp = x_prompt.reshape(bp * tp, D_MODEL)
    xs = x_sample.reshape(bs * ts, D_MODEL)
    final_g = norm_final_g.reshape(1, D_MODEL)
    fwd_states = []
    bwd_states = []
    for l in range(depth):
        last = l == depth - 1
        mod_ctx = jnp.broadcast_to(mod[l, bs][None, None, :], (bp, 1, 6 * D_MODEL))
        mod_lat = mod[l, :bs][:, None, :]
        xp, sf, sb = _layer(xp, mod_ctx, None, None, wts, l, bp, tp, tp, True, final_g if last else None)
        fwd_states.append(sf)
        bwd_states.append(sb)
        xs, _, _ = _layer(xs, mod_lat, state_gla_fwd, state_gla_bwd, wts, l, bs, ts, GRID_W, False,
                          final_g if last else None)
    y_prompt = xp.reshape(bp, tp, D_MODEL)
    y_sample = xs.reshape(bs, ts, D_MODEL)
    return (y_prompt, y_sample, jnp.concatenate(fwd_states, axis=1), jnp.concatenate(bwd_states, axis=1))
```

```python
import functools

import jax
import jax.numpy as jnp
from jax import lax
from jax.experimental import pallas as pl
from jax.experimental.pallas import tpu as pltpu

F32 = jnp.float32
BF16 = jnp.bfloat16
I32 = jnp.int32

D_MODEL = 1024
GRID_W = 64
MIX_GLA = 512
MIX_CONV = 512
GLA_HEADS = 4
GLA_DK = 64
GLA_DV = 128
GLA_KEY = GLA_HEADS * GLA_DK
GATE_RANK = 16
GATE_TAU = 16.0
CHUNK = 64
CONV_W = 31
CONV_HALF = CONV_W // 2
N_EXPERTS = 16
EC_FACTOR = 2
D_EXPERT = 1024
EPS = 1e-6

LANES = 128
SUBLANES = 8

PROJ_COLS = 2 * GLA_KEY + 2 * MIX_GLA + 2 * MIX_CONV + LANES
COLBLK = 512
Z_COLBLK = (2 * GLA_KEY + 2 * MIX_GLA + 2 * MIX_CONV) // LANES

VMEM_LIMIT = 56 * 1024 * 1024
CONV_PAD = 16
CONV_CH = 64
OUTPROJ_SUB = 128
GLA_GROUP = 4


def _cparams():
    return pltpu.CompilerParams(vmem_limit_bytes=VMEM_LIMIT)


def _silu(x):
    return x * jax.nn.sigmoid(x)


def _rms(x):
    return x * lax.rsqrt(jnp.mean(x * x, axis=-1, keepdims=True) + EPS)


def _dot(a, b):
    return jnp.dot(a, b, preferred_element_type=F32)


def _mod_kernel(c_ref, w_ref, b_ref, o_ref):
    s = _silu(c_ref[...]).astype(BF16)
    o_ref[0] = _dot(s, w_ref[0].astype(BF16)) + b_ref[0]


def _modulation(cond, w_mod, b_mod):
    depth = w_mod.shape[0]
    n_out = w_mod.shape[2]
    tn = 1536
    rows = cond.shape[0]
    return pl.pallas_call(
        _mod_kernel,
        grid=(depth, n_out // tn),
        in_specs=[
            pl.BlockSpec((rows, D_MODEL), lambda l, j: (0, 0)),
            pl.BlockSpec((1, D_MODEL, tn), lambda l, j: (l, 0, j)),
            pl.BlockSpec((1, 1, tn), lambda l, j: (l, 0, j)),
        ],
        out_specs=pl.BlockSpec((1, rows, tn), lambda l, j: (l, 0, j)),
        out_shape=jax.ShapeDtypeStruct((depth, rows, n_out), F32),
        compiler_params=_cparams(),
        name="modulation",
    )(cond, w_mod, b_mod.reshape(depth, 1, n_out))


def _inproj_kernel(x_ref, mod_ref, g_ref, w_ref, o_ref):
    m = mod_ref[0]
    sh = m[:, 0:D_MODEL]
    sc = m[:, D_MODEL:2 * D_MODEL]
    h = (_rms(x_ref[...]) * g_ref[0]) * (1.0 + sc) + sh
    o_ref[...] = _dot(h.astype(BF16), w_ref[0])


def _inproj(x, modp, norm_g, w_in_b, layer, bsz, t):
    tm = min(t, 512)
    nt = t // tm
    return pl.pallas_call(
        _inproj_kernel,
        grid=(bsz, nt),
        in_specs=[
            pl.BlockSpec((tm, D_MODEL), lambda b, i: (b * nt + i, 0)),
            pl.BlockSpec((1, 1, 6 * D_MODEL), lambda b, i: (b, 0, 0)),
            pl.BlockSpec((1, 1, D_MODEL), lambda b, i: (layer, 0, 0)),
            pl.BlockSpec((1, D_MODEL, PROJ_COLS), lambda b, i: (layer, 0, 0)),
        ],
        out_specs=pl.BlockSpec((tm, PROJ_COLS), lambda b, i: (b * nt + i, 0)),
        out_shape=jax.ShapeDtypeStruct((bsz * t, PROJ_COLS), F32),
        compiler_params=_cparams(),
        name="inproj",
    )(x, modp, norm_g, w_in_b)


def _log_sigmoid(x):
    return jnp.minimum(x, 0.0) - jnp.log(1.0 + jnp.exp(-jnp.abs(x)))


def _head_stack(x, lane_head):
    return jnp.concatenate([jnp.where(lane_head == h, x, 0.0) for h in range(GLA_HEADS)], axis=0).astype(BF16)


def _gla_group(chains, lane_head, s_ref, rhs_ref):
    cums = []
    for qk, v, g, tri, causal, d, last_row in chains:
        g_hi = g.astype(BF16)
        g_lo = (g - g_hi.astype(F32)).astype(BF16)
        cums.append(_dot(tri, g_hi) + _dot(tri, g_lo))
    pre = []
    for (qk, v, g, tri, causal, d, last_row), cum in zip(chains, cums):
        cl = cum[last_row:last_row + 1, :]
        q = qk[:, 0:GLA_KEY]
        k = qk[:, GLA_KEY:2 * GLA_KEY]
        qd = (q * (jnp.exp(cum) * (GLA_DK ** -0.5))).astype(BF16)
        k_inv = k * jnp.exp(-cum)
        k_end = k_inv * jnp.exp(cl)
        tr = jnp.concatenate([k_end, cum], axis=0).T
        pre.append((qd, _head_stack(k_inv, lane_head), tr, v.astype(BF16)))
    mm = []
    for (qk, v, g, tri, causal, d, last_row), (qd, ks, tr, vb) in zip(chains, pre):
        sc = lax.dot_general(qd, ks, (((1,), (1,)), ((), ())), preferred_element_type=F32)
        upd = [_dot(tr[h * GLA_DK:(h + 1) * GLA_DK, 0:CHUNK].astype(BF16),
                    vb[:, h * GLA_DV:(h + 1) * GLA_DV]) for h in range(GLA_HEADS)]
        mm.append((sc, jnp.concatenate(upd, axis=0)))
    state = {}
    lhs = []
    for i, ((qk, v, g, tri, causal, d, last_row), (qd, ks, tr, vb), (sc, upd)) in enumerate(zip(chains, pre, mm)):
        s_prev = state[d] if d in state else s_ref[d]
        s_prev_b = s_prev.astype(BF16)
        for h in range(GLA_HEADS):
            rows = slice(h * GLA_DK, (h + 1) * GLA_DK)
            cols = slice(h * GLA_DV, (h + 1) * GLA_DV)
            rhs_ref[i, rows, cols] = vb[:, cols]
            rhs_ref[i, GLA_KEY + h * GLA_DK:GLA_KEY + (h + 1) * GLA_DK, cols] = s_prev_b[rows, :]
        decay = jnp.exp(tr[:, CHUNK + last_row:CHUNK + last_row + 1])
        state[d] = decay * s_prev + upd
        lhs.append(jnp.concatenate([(sc * causal).astype(BF16), qd], axis=1))
    outs = [_dot(l, rhs_ref[i]) for i, l in enumerate(lhs)]
    for d, s_new in state.items():
        s_ref[d] = s_new
    return outs


def _gla_kernel(*refs, t, zero_init, emit_state):
    qk_ref, v_ref, z_ref, wf_ref, bf_ref, wb_ref, bb_ref = refs[:7]
    pos = 7
    if not zero_init:
        s0f_ref, s0b_ref = refs[pos:pos + 2]
        pos += 2
    o_ref = refs[pos]
    pos += 1
    if emit_state:
        sf_ref, sb_ref = refs[pos:pos + 2]
        pos += 2
    gf_s, gb_s, s_ref, mask_s, rhs_s = refs[pos:pos + 5]
    rhs_s[...] = jnp.zeros(rhs_s.shape, BF16)

    z = z_ref[...].astype(BF16)
    gf_s[...] = _log_sigmoid(_dot(z, wf_ref[0]) + bf_ref[0]) * (1.0 / GATE_TAU)
    gb_s[...] = _log_sigmoid(_dot(z, wb_ref[0]) + bb_ref[0]) * (1.0 / GATE_TAU)
    if zero_init:
        s_ref[...] = jnp.zeros(s_ref.shape, F32)
    else:
        s_ref[0] = s0f_ref[0, 0].reshape(GLA_KEY, GLA_DV)
        s_ref[1] = s0b_ref[0, 0].reshape(GLA_KEY, GLA_DV)

    ri = lax.broadcasted_iota(I32, (CHUNK, CHUNK), 0)
    ci = lax.broadcasted_iota(I32, (CHUNK, CHUNK), 1)
    tri_low = jnp.where(ri >= ci, 1.0, 0.0).astype(BF16)
    tri_upp = jnp.where(ri <= ci, 1.0, 0.0).astype(BF16)
    rs = lax.broadcasted_iota(I32, (CHUNK, GLA_HEADS * CHUNK), 0)
    cs = lax.broadcasted_iota(I32, (CHUNK, GLA_HEADS * CHUNK), 1) % CHUNK
    mask_s[0] = jnp.where(rs >= cs, 1.0, 0.0)
    mask_s[1] = jnp.where(rs <= cs, 1.0, 0.0)
    lane_head = lax.broadcasted_iota(I32, (CHUNK, GLA_KEY), 1) // GLA_DK
    n = t // CHUNK

    group = min(GLA_GROUP, n // 2)

    def step(i, first):
        rows = []
        chains = []
        for j in range(group):
            c = i * group + j
            rf = pl.multiple_of(c * CHUNK, CHUNK)
            rb = pl.multiple_of((n - 1 - c) * CHUNK, CHUNK)
            rows += [rf, rb]
            chains.append((qk_ref[pl.ds(rf, CHUNK), :], v_ref[pl.ds(rf, CHUNK), :],
                           gf_s[pl.ds(rf, CHUNK), :], tri_low, mask_s[0], 0, CHUNK - 1))
            chains.append((qk_ref[pl.ds(rb, CHUNK), :], v_ref[pl.ds(rb, CHUNK), :],
                           gb_s[pl.ds(rb, CHUNK), :], tri_upp, mask_s[1], 1, 0))
        outs = _gla_group(chains, lane_head, s_ref, rhs_s)
        for r, o in zip(rows, outs):
            if first:
                o_ref[pl.ds(r, CHUNK), :] = o
            else:
                o_ref[pl.ds(r, CHUNK), :] += o

    def first_half(i, c):
        step(i, True)
        return c

    def second_half(i, c):
        step(i, False)
        return c

    half = n // (2 * group)
    lax.fori_loop(0, half, first_half, 0)
    lax.fori_loop(half, 2 * half, second_half, 0)
    if emit_state:
        sf_ref[0, 0] = s_ref[0].reshape(GLA_HEADS, GLA_DK, GLA_DV)
        sb_ref[0, 0] = s_ref[1].reshape(GLA_HEADS, GLA_DK, GLA_DV)


def _gla(proj, wdf, bdf, wdb, bdb, s0f, s0b, layer, bsz, t, emit_state):
    zero_init = s0f is None
    st_block = (1, 1, GLA_HEADS, GLA_DK, GLA_DV)
    in_specs = [
        pl.BlockSpec((t, COLBLK), lambda b: (b, 0)),
        pl.BlockSpec((t, COLBLK), lambda b: (b, 1)),
        pl.BlockSpec((t, LANES), lambda b: (b, Z_COLBLK)),
        pl.BlockSpec((1, LANES, GLA_KEY), lambda b: (layer, 0, 0)),
        pl.BlockSpec((1, 1, GLA_KEY), lambda b: (layer, 0, 0)),
        pl.BlockSpec((1, LANES, GLA_KEY), lambda b: (layer, 0, 0)),
        pl.BlockSpec((1, 1, GLA_KEY), lambda b: (layer, 0, 0)),
    ]
    args = [proj, proj, proj, wdf, bdf, wdb, bdb]
    if not zero_init:
        in_specs += [pl.BlockSpec(st_block, lambda b: (b, layer, 0, 0, 0))] * 2
        args += [s0f, s0b]
    out_specs = [pl.BlockSpec((t, MIX_GLA), lambda b: (b, 0))]
    out_shape = [jax.ShapeDtypeStruct((bsz * t, MIX_GLA), F32)]
    if emit_state:
        out_specs += [pl.BlockSpec((1, 1, GLA_HEADS, GLA_DK, GLA_DV), lambda b: (b, 0, 0, 0, 0))] * 2
        out_shape += [jax.ShapeDtypeStruct((bsz, 1, GLA_HEADS, GLA_DK, GLA_DV), F32)] * 2
    return pl.pallas_call(
        functools.partial(_gla_kernel, t=t, zero_init=zero_init, emit_state=emit_state),
        grid=(bsz,),
        in_specs=in_specs,
        out_specs=out_specs,
        out_shape=out_shape,
        scratch_shapes=[
            pltpu.VMEM((t, GLA_KEY), F32),
            pltpu.VMEM((t, GLA_KEY), F32),
            pltpu.VMEM((2, GLA_KEY, GLA_DV), F32),
            pltpu.VMEM((2, CHUNK, GLA_HEADS * CHUNK), F32),
            pltpu.VMEM((2 * min(GLA_GROUP, t // CHUNK // 2), 2 * GLA_KEY, MIX_GLA), BF16),
        ],
        compiler_params=_cparams(),
        name="gla",
    )(*args)


def _conv_kernel(ga_ref, gg_ref, w_ref, b_ref, lg_ref, lb_ref, o_ref, pad_ref, y_ref, *, nrows, rlen):
    stride = rlen + 2 * CONV_PAD
    ntile = MIX_CONV // LANES
    u = ga_ref[...] * jax.nn.sigmoid(gg_ref[...])
    zeros = jnp.zeros((CONV_PAD, LANES), F32)
    for lt in range(ntile):
        for r in range(nrows):
            base = r * stride
            pad_ref[lt, base:base + CONV_PAD, :] = zeros
            pad_ref[lt, base + CONV_PAD:base + CONV_PAD + rlen, :] = u[r * rlen:(r + 1) * rlen,
                                                                       lt * LANES:(lt + 1) * LANES]
            pad_ref[lt, base + CONV_PAD + rlen:base + stride, :] = zeros
    lg = lg_ref[0]
    lb = lb_ref[0]
    assert rlen == CONV_CH or nrows == 1
    pstep = stride if rlen == CONV_CH else CONV_CH
    off = CONV_PAD - CONV_HALF
    ngrp = CONV_CH // SUBLANES
    nshift = (CONV_CH + 2 * CONV_PAD) // SUBLANES - 1

    def chunk(c, carry):
        p0 = pl.multiple_of(c * pstep, SUBLANES)
        g0 = c * ngrp

        def lane_tile(lt, carry2):
            win = pad_ref[lt, pl.ds(p0, CONV_CH + 2 * CONV_PAD), :]
            acc = None
            for s in range(SUBLANES):
                xs = win if s == 0 else win[s:s + nshift * SUBLANES, :]
                xs = xs.reshape(-1, SUBLANES, LANES)
                for a in range(2 * CONV_PAD // SUBLANES):
                    j = SUBLANES * a + s - off
                    if 0 <= j < CONV_W:
                        term = xs[a:a + ngrp] * w_ref[0, lt, j][None]
                        acc = term if acc is None else acc + term
            y_ref[lt, pl.ds(g0, ngrp)] = acc + b_ref[0, lt][None]
            return carry2

        lax.fori_loop(0, ntile, lane_tile, 0)
        return carry

    lax.fori_loop(0, nrows * rlen // CONV_CH, chunk, 0)
    y = jnp.concatenate([y_ref[lt].reshape(nrows * rlen, LANES) for lt in range(ntile)], axis=1)
    yc = y - jnp.mean(y, axis=-1, keepdims=True)
    yn = yc * lax.rsqrt(jnp.mean(yc * yc, axis=-1, keepdims=True) + EPS)
    o_ref[...] = _silu(yn * lg + lb)


def _conv(proj, conv_w, conv_b, ln_g, ln_b, layer, bsz, t, rlen):
    tb = max(rlen, min(t, 512))
    nrows = tb // rlen
    nt = t // tb
    ntile = MIX_CONV // LANES
    vec = pl.BlockSpec((1, 1, MIX_CONV), lambda b, i: (layer, 0, 0))
    return pl.pallas_call(
        functools.partial(_conv_kernel, nrows=nrows, rlen=rlen),
        grid=(bsz, nt),
        in_specs=[
            pl.BlockSpec((tb, COLBLK), lambda b, i: (b * nt + i, 3)),
            pl.BlockSpec((tb, COLBLK), lambda b, i: (b * nt + i, 4)),
            pl.BlockSpec((1, ntile, CONV_W, SUBLANES, LANES), lambda b, i: (layer, 0, 0, 0, 0)),
            pl.BlockSpec((1, ntile, SUBLANES, LANES), lambda b, i: (layer, 0, 0, 0)),
            vec, vec,
        ],
        out_specs=pl.BlockSpec((tb, MIX_CONV), lambda b, i: (b * nt + i, 0)),
        out_shape=jax.ShapeDtypeStruct((bsz * t, MIX_CONV), F32),
        scratch_shapes=[
            pltpu.VMEM((ntile, nrows * (rlen + 2 * CONV_PAD), LANES), F32),
            pltpu.VMEM((ntile, tb // SUBLANES, SUBLANES, LANES), F32),
        ],
        compiler_params=_cparams(),
        name="conv",
    )(proj, proj, conv_w, conv_b, ln_g, ln_b)


def _outproj_kernel(o_ref, r_ref, u_ref, x_ref, mod_ref, gn_ref, wo_ref, g2_ref, wr_ref,
                    xo_ref, h2_ref, aff_ref):
    gn = gn_ref[0]
    m = mod_ref[0]
    gt1 = m[:, 2 * D_MODEL:3 * D_MODEL]
    sh2 = m[:, 3 * D_MODEL:4 * D_MODEL]
    sc2 = m[:, 4 * D_MODEL:5 * D_MODEL]
    tm = o_ref.shape[0]
    subs = [slice(r0, r0 + OUTPROJ_SUB) for r0 in range(0, tm, OUTPROJ_SUB)]
    acts = []
    for rs in subs:
        o = o_ref[rs, :]
        r = r_ref[rs, :]
        parts = []
        for h in range(GLA_HEADS):
            hs = slice(h * GLA_DV, (h + 1) * GLA_DV)
            parts.append((_rms(o[:, hs]) * gn) * _silu(r[:, hs]))
        parts.append(u_ref[rs, :])
        acts.append(jnp.concatenate(parts, axis=1).astype(BF16))
    mixed = [_dot(a, wo_ref[0]) for a in acts]
    h2s = []
    for rs, mix in zip(subs, mixed):
        x = x_ref[rs, :] + gt1 * mix
        xo_ref[rs, :] = x
        h2 = ((_rms(x) * g2_ref[0]) * (1.0 + sc2) + sh2).astype(BF16)
        h2_ref[rs, :] = h2
        h2s.append(h2)
    logit = [_dot(h2, wr_ref[0]) for h2 in h2s]
    for rs, logits in zip(subs, logit):
        lane = lax.broadcasted_iota(I32, logits.shape, 1)
        logits = jnp.where(lane < N_EXPERTS, logits, -jnp.inf)
        ex = jnp.exp(logits - jnp.max(logits, axis=-1, keepdims=True))
        aff_ref[rs, :] = ex / jnp.sum(ex, axis=-1, keepdims=True)


def _outproj(o, proj, u, x, modp, gla_norm_g, w_out_b, norm_ffn_g, w_router_b, layer, bsz, t):
    tm = min(t, 512)
    nt = t // tm
    row = lambda b, i: (b * nt + i, 0)
    return pl.pallas_call(
        _outproj_kernel,
        grid=(bsz, nt),
        in_specs=[
            pl.BlockSpec((tm, MIX_GLA), row),
            pl.BlockSpec((tm, COLBLK), lambda b, i: (b * nt + i, 2)),
            pl.BlockSpec((tm, MIX_CONV), row),
            pl.BlockSpec((tm, D_MODEL), row),
            pl.BlockSpec((1, 1, 6 * D_MODEL), lambda b, i: (b, 0, 0)),
            pl.BlockSpec((1, 1, GLA_DV), lambda b, i: (layer, 0, 0)),
            pl.BlockSpec((1, D_MODEL, D_MODEL), lambda b, i: (layer, 0, 0)),
            pl.BlockSpec((1, 1, D_MODEL), lambda b, i: (layer, 0, 0)),
            pl.BlockSpec((1, D_MODEL, LANES), lambda b, i: (layer, 0, 0)),
        ],
        out_specs=[
            pl.BlockSpec((tm, D_MODEL), row),
            pl.BlockSpec((tm, D_MODEL), row),
            pl.BlockSpec((tm, LANES), row),
        ],
        out_shape=[
            jax.ShapeDtypeStruct((bsz * t, D_MODEL), F32),
            jax.ShapeDtypeStruct((bsz * t, D_MODEL), BF16),
            jax.ShapeDtypeStruct((bsz * t, LANES), F32),
        ],
        compiler_params=_cparams(),
        name="outproj",
    )(o, proj, u, x, modp, gla_norm_g, w_out_b, norm_ffn_g, w_router_b)


PREFIX_BLK = 256


def _excl_prefix(x, t):
    blk = min(t, PREFIX_BLK)
    ri = lax.broadcasted_iota(I32, (blk, blk), 0)
    ci = lax.broadcasted_iota(I32, (blk, blk), 1)
    upper = jnp.where(ri < ci, 1.0, 0.0).astype(BF16)
    run = jnp.zeros((x.shape[0], 1), F32)
    outs = []
    for j in range(t // blk):
        xb = x[:, j * blk:(j + 1) * blk]
        outs.append(_dot(xb.astype(BF16), upper) + run)
        run = run + jnp.sum(xb, axis=1, keepdims=True)
    return jnp.concatenate(outs, axis=1) if len(outs) > 1 else outs[0]


F32_MAGNITUDE_BITS = 31
REFINE_STEPS = 16


def _route_kernel(aff_ref, sel_ref, selt_ref, afft_ref, at_s, *, bsz, t, cap):
    for b in range(bsz):
        at_s[b * N_EXPERTS:(b + 1) * N_EXPERTS, :] = aff_ref[b * t:(b + 1) * t, :].T[0:N_EXPERTS, :]
    a = at_s[...]
    afft_ref[...] = a
    rows = bsz * N_EXPERTS
    capf = float(cap)

    def enough(th):
        return jnp.sum(jnp.where(a >= th, 1.0, 0.0), axis=1, keepdims=True) >= capf

    def bit_step(i, thr):
        cand = thr | jnp.left_shift(jnp.int32(1), F32_MAGNITUDE_BITS - 1 - i)
        return jnp.where(enough(lax.bitcast_convert_type(cand, F32)), cand, thr)

    thr = lax.fori_loop(0, F32_MAGNITUDE_BITS, bit_step, jnp.zeros((rows, 1), I32))
    lo = lax.bitcast_convert_type(thr, F32)
    hi = lax.bitcast_convert_type(thr + 1, F32)

    def refine(i, c):
        lo, hi = c
        mid = lo + (hi - lo) * 0.5
        ok = enough(mid)
        return jnp.where(ok, mid, lo), jnp.where(ok, hi, mid)

    lo, hi = lax.fori_loop(0, REFINE_STEPS, refine, (lo, hi))
    gt = jnp.where(a >= hi, 1.0, 0.0) * jnp.where(a > lo, 1.0, 0.0)
    eq = jnp.where(a >= lo, 1.0, 0.0) - gt
    need = capf - jnp.sum(gt, axis=1, keepdims=True)
    tie_rank = _excl_prefix(eq, t)
    m = gt + eq * jnp.where(tie_rank < need, 1.0, 0.0)
    slot = _excl_prefix(m, t)
    sel = jnp.where(m > 0.5, slot.astype(I32), -1)
    sel_ref[...] = sel
    filler = jnp.full((LANES - N_EXPERTS, t), -1, I32)
    for b in range(bsz):
        selt_ref[b * t:(b + 1) * t, :] = jnp.concatenate(
            [sel[b * N_EXPERTS:(b + 1) * N_EXPERTS, :], filler], axis=0).T


def _route(aff, bsz, t, cap):
    rows = bsz * N_EXPERTS
    sel, selt, afft = pl.pallas_call(
        functools.partial(_route_kernel, bsz=bsz, t=t, cap=cap),
        grid=(1,),
        in_specs=[pl.BlockSpec((bsz * t, LANES), lambda i: (0, 0))],
        out_specs=[
            pl.BlockSpec((rows, t), lambda i: (0, 0)),
            pl.BlockSpec((bsz * t, LANES), lambda i: (0, 0)),
            pl.BlockSpec((rows, t), lambda i: (0, 0)),
        ],
        out_shape=[
            jax.ShapeDtypeStruct((rows, t), I32),
            jax.ShapeDtypeStruct((bsz * t, LANES), I32),
            jax.ShapeDtypeStruct((rows, t), F32),
        ],
        scratch_shapes=[pltpu.VMEM((rows, t), F32)],
        compiler_params=_cparams(),
        name="route",
    )(aff)
    return sel.reshape(bsz, N_EXPERTS, t), selt, afft.reshape(bsz, N_EXPERTS, t)


def _gather_kernel(h_ref, sel_ref, afft_ref, xs_ref, gate_ref, *, t, cap, eg):
    g = pl.program_id(1)
    hb = h_ref[...]
    slot = lax.broadcasted_iota(I32, (cap, t), 0)
    for el in range(eg):
        if eg == N_EXPERTS:
            srow = sel_ref[0, el:el + 1, :]
            arow = afft_ref[0, el:el + 1, :]
        else:
            e = g * eg + el
            srow = sel_ref[0, pl.ds(e, 1), :]
            arow = afft_ref[0, pl.ds(e, 1), :]
        hit = srow == slot
        p = jnp.where(hit, 1.0, 0.0).astype(BF16)
        xs_ref[el, 0] = _dot(p, hb).astype(BF16)
        gate_ref[el, 0] = jnp.sum(jnp.where(hit, arow, 0.0), axis=1, keepdims=True)


def _gather(h2, sel, afft, bsz, t, cap):
    eg = max(1, min(N_EXPERTS, 512 // cap))
    return pl.pallas_call(
        functools.partial(_gather_kernel, t=t, cap=cap, eg=eg),
        grid=(bsz, N_EXPERTS // eg),
        in_specs=[
            pl.BlockSpec((t, D_MODEL), lambda b, g: (b, 0)),
            pl.BlockSpec((1, N_EXPERTS, t), lambda b, g: (b, 0, 0)),
            pl.BlockSpec((1, N_EXPERTS, t), lambda b, g: (b, 0, 0)),
        ],
        out_specs=[
            pl.BlockSpec((eg, 1, cap, D_MODEL), lambda b, g: (g, b, 0, 0)),
            pl.BlockSpec((eg, 1, cap, 1), lambda b, g: (g, b, 0, 0)),
        ],
        out_shape=[
            jax.ShapeDtypeStruct((N_EXPERTS, bsz, cap, D_MODEL), BF16),
            jax.ShapeDtypeStruct((N_EXPERTS, bsz, cap, 1), F32),
        ],
        compiler_params=_cparams(),
        name="gather",
    )(h2, sel, afft)


FFN_TF = 256
FFN_TM = 512


def _ffn_kernel(xc_ref, xl_ref, gc_ref, gl_ref, wg_ref, wu_ref, wd_ref, yc_ref, yl_ref, acc_ref):
    def step(first):
        wg = wg_ref[0, 0].astype(BF16)
        wu = wu_ref[0, 0].astype(BF16)
        wd = wd_ref[0, 0].astype(BF16)
        base = 0
        for x_ref, g_ref, y_ref in ((xc_ref, gc_ref, yc_ref), (xl_ref, gl_ref, yl_ref)):
            rows = x_ref.shape[1]
            for r0 in range(0, rows, FFN_TM):
                x = x_ref[0, r0:r0 + FFN_TM, :]
                hm = (_silu(_dot(x, wg)) * _dot(x, wu)).astype(BF16)
                part = _dot(hm, wd)
                if first:
                    acc_ref[base + r0:base + r0 + FFN_TM, :] = part
                else:
                    acc = acc_ref[base + r0:base + r0 + FFN_TM, :] + part
                    acc_ref[base + r0:base + r0 + FFN_TM, :] = acc
                    y_ref[0, r0:r0 + FFN_TM, :] = (acc * g_ref[0, r0:r0 + FFN_TM, :]).astype(BF16)
            base += rows

    j = pl.program_id(1)
    pl.when(j == 0)(functools.partial(step, True))
    pl.when(j > 0)(functools.partial(step, False))


def _ffn(xs_c, gates_c, xs_l, gates_l, w_e_gate, w_e_up, w_e_down, layer):
    rc = xs_c.shape[1]
    rl = xs_l.shape[1]
    assert rc % FFN_TM == 0 and rl % FFN_TM == 0
    rowblk = lambda r, w: pl.BlockSpec((1, r, w), lambda e, j: (e, 0, 0))
    return pl.pallas_call(
        _ffn_kernel,
        grid=(N_EXPERTS, D_EXPERT // FFN_TF),
        in_specs=[
            rowblk(rc, D_MODEL), rowblk(rl, D_MODEL), rowblk(rc, 1), rowblk(rl, 1),
            pl.BlockSpec((1, 1, D_MODEL, FFN_TF), lambda e, j: (layer, e, 0, j)),
            pl.BlockSpec((1, 1, D_MODEL, FFN_TF), lambda e, j: (layer, e, 0, j)),
            pl.BlockSpec((1, 1, FFN_TF, D_MODEL), lambda e, j: (layer, e, j, 0)),
        ],
        out_specs=[rowblk(rc, D_MODEL), rowblk(rl, D_MODEL)],
        out_shape=[
            jax.ShapeDtypeStruct((N_EXPERTS, rc, D_MODEL), BF16),
            jax.ShapeDtypeStruct((N_EXPERTS, rl, D_MODEL), BF16),
        ],
        scratch_shapes=[pltpu.VMEM((rc + rl, D_MODEL), F32)],
        compiler_params=_cparams(),
        name="ffn",
    )(xs_c, xs_l, gates_c, gates_l, w_e_gate, w_e_up, w_e_down)


def _scatter_kernel(*refs, cap, final):
    if final:
        selt_ref, y_ref, x_ref, mod_ref, gf_ref, xo_ref, pt_s = refs
    else:
        selt_ref, y_ref, x_ref, mod_ref, xo_ref, pt_s = refs
    tt = selt_ref.shape[0]
    ncol = N_EXPERTS * cap
    gw = min(ncol, 1024)
    stf = selt_ref[...].astype(F32).astype(BF16)
    col = lax.broadcasted_iota(I32, (LANES, gw), 1)
    row = lax.broadcasted_iota(I32, (LANES, gw), 0)
    cmod = (lax.broadcasted_iota(I32, (1, gw), 1) % cap).astype(F32)
    for g in range(ncol // gw):
        expand = jnp.where((col + g * gw) // cap == row, 1.0, 0.0).astype(BF16)
        selx = _dot(stf, expand)
        pt_s[:, g * gw:(g + 1) * gw] = jnp.where(selx == cmod, 1.0, 0.0).astype(BF16)
    yb = y_ref[:, 0].reshape(ncol, D_MODEL)
    gt2 = mod_ref[0][:, 5 * D_MODEL:6 * D_MODEL]
    x = x_ref[...] + gt2 * _dot(pt_s[...], yb)
    if final:
        x = _rms(x) * gf_ref[...]
    xo_ref[...] = x


def _scatter(selt, y, x, modp, norm_final_g, bsz, t, cap):
    final = norm_final_g is not None
    tt = min(t, 512)
    nt = t // tt
    row = lambda b, i: (b * nt + i, 0)
    in_specs = [
        pl.BlockSpec((tt, LANES), row),
        pl.BlockSpec((N_EXPERTS, 1, cap, D_MODEL), lambda b, i: (0, b, 0, 0)),
        pl.BlockSpec((tt, D_MODEL), row),
        pl.BlockSpec((1, 1, 6 * D_MODEL), lambda b, i: (b, 0, 0)),
    ]
    args = [selt, y, x, modp]
    if final:
        in_specs.append(pl.BlockSpec((1, D_MODEL), lambda b, i: (0, 0)))
        args.append(norm_final_g)
    return pl.pallas_call(
        functools.partial(_scatter_kernel, cap=cap, final=final),
        grid=(bsz, nt),
        in_specs=in_specs,
        out_specs=pl.BlockSpec((tt, D_MODEL), row),
        out_shape=jax.ShapeDtypeStruct((bsz * t, D_MODEL), F32),
        scratch_shapes=[pltpu.VMEM((tt, N_EXPERTS * cap), BF16)],
        compiler_params=_cparams(),
        name="scatter",
    )(*args)


def _mixer_and_route(x, modp, s0f, s0b, wts, layer, bsz, t, rlen, emit_state):
    cap = EC_FACTOR * t // N_EXPERTS
    proj = _inproj(x, modp, wts["norm_mix_g"], wts["w_in"], layer, bsz, t)
    gla = _gla(proj, wts["wdf"], wts["bdf"], wts["wdb"], wts["bdb"], s0f, s0b, layer, bsz, t, emit_state)
    if emit_state:
        o, sf, sb = gla
    else:
        (o,) = gla
        sf = sb = None
    u = _conv(proj, wts["conv_w"], wts["conv_b"], wts["conv_ln_g"], wts["conv_ln_b"], layer, bsz, t, rlen)
    x, h2, aff = _outproj(o, proj, u, x, modp, wts["gla_norm_g"], wts["w_out"], wts["norm_ffn_g"],
                          wts["w_router"], layer, bsz, t)
    sel, selt, afft = _route(aff, bsz, t, cap)
    xs, gates = _gather(h2, sel, afft, bsz, t, cap)
    return (x, selt, xs.reshape(N_EXPERTS, bsz * cap, D_MODEL), gates.reshape(N_EXPERTS, bsz * cap, 1)), sf, sb


def kernel(x_prompt, x_sample, state_gla_fwd, state_gla_bwd, c, c_ctx, norm_mix_g, norm_ffn_g, norm_final_g, w_mod, b_mod, w_in, w_decay_f, b_decay_f, w_decay_b, b_decay_b, gla_norm_g, conv_w, conv_b, conv_ln_g, conv_ln_b, w_out, w_router, w_e_gate, w_e_up, w_e_down):
    depth = w_in.shape[0]
    bp, tp, _ = x_prompt.shape
    bs, ts, _ = x_sample.shape

    zcols = 2 * GLA_KEY + 2 * MIX_GLA
    w_in_r = jnp.concatenate(
        [w_in[:, :, :zcols], w_in[:, :, zcols + 2 * GATE_RANK:], w_in[:, :, zcols:zcols + 2 * GATE_RANK],
         jnp.zeros((depth, D_MODEL, LANES - 2 * GATE_RANK), F32)], axis=2).astype(BF16)
    zpad_f = jnp.zeros((depth, LANES - GATE_RANK, GLA_KEY), F32)
    zpad_b0 = jnp.zeros((depth, GATE_RANK, GLA_KEY), F32)
    zpad_b1 = jnp.zeros((depth, LANES - 2 * GATE_RANK, GLA_KEY), F32)
    vec = lambda a: a.reshape(depth, 1, a.shape[-1])
    wts = dict(
        norm_mix_g=vec(norm_mix_g), norm_ffn_g=vec(norm_ffn_g), w_in=w_in_r,
        wdf=jnp.concatenate([w_decay_f, zpad_f], axis=1).astype(BF16), bdf=vec(b_decay_f),
        wdb=jnp.concatenate([zpad_b0, w_decay_b, zpad_b1], axis=1).astype(BF16), bdb=vec(b_decay_b),
        gla_norm_g=vec(gla_norm_g),
        conv_w=jnp.broadcast_to(
            conv_w.reshape(depth, CONV_W, 1, MIX_CONV // LANES, LANES).transpose(0, 3, 1, 2, 4),
            (depth, MIX_CONV // LANES, CONV_W, SUBLANES, LANES)),
        conv_b=jnp.broadcast_to(conv_b.reshape(depth, MIX_CONV // LANES, 1, LANES),
                                (depth, MIX_CONV // LANES, SUBLANES, LANES)),
        conv_ln_g=vec(conv_ln_g), conv_ln_b=vec(conv_ln_b), w_out=w_out.astype(BF16),
        w_router=jnp.concatenate(
            [w_router, jnp.zeros((depth, D_MODEL, LANES - N_EXPERTS), F32)], axis=2).astype(BF16),
        w_e_gate=w_e_gate, w_e_up=w_e_up, w_e_down=w_e_down,
    )

    nrow = 2 * SUBLANES
    cond = jnp.concatenate([c, c_ctx[None, :], jnp.zeros((nrow - bs - 1, D_MODEL), F32)], axis=0)
    mod = _modulation(cond, w_mod, b_mod)

    xp = x_prompt.reshape(bp * tp, D_MODEL)
    xs = x_sample.reshape(bs * ts, D_MODEL)
    final_g = norm_final_g.reshape(1, D_MODEL)
    fwd_states = []
    bwd_states = []
    for l in range(depth):
        last = l == depth - 1
        mod_ctx = jnp.broadcast_to(mod[l, bs][None, None, :], (bp, 1, 6 * D_MODEL))
        mod_lat = mod[l, :bs][:, None, :]
        (xp, selt_c, xs_c, gates_c), sf, sb = _mixer_and_route(
            xp, mod_ctx, None, None, wts, l, bp, tp, tp, True)
        fwd_states.append(sf)
        bwd_states.append(sb)
        (xs, selt_l, xs_l, gates_l), _, _ = _mixer_and_route(
            xs, mod_lat, state_gla_fwd, state_gla_bwd, wts, l, bs, ts, GRID_W, False)
        y_c, y_l = _ffn(xs_c, gates_c, xs_l, gates_l, wts["w_e_gate"], wts["w_e_up"], wts["w_e_down"], l)
        cap_c = EC_FACTOR * tp // N_EXPERTS
        cap_l = EC_FACTOR * ts // N_EXPERTS
        fin = final_g if last else None
        xp = _scatter(selt_c, y_c.reshape(N_EXPERTS, bp, cap_c, D_MODEL), xp, mod_ctx, fin, bp, tp, cap_c)
        xs = _scatter(selt_l, y_l.reshape(N_EXPERTS, bs, cap_l, D_MODEL), xs, mod_lat, fin, bs, ts, cap_l)
    y_prompt = xp.reshape(bp, tp, D_MODEL)
    y_sample = xs.reshape(bs, ts, D_MODEL)
    return (y_prompt, y_sample, jnp.concatenate(fwd_states, axis=1), jnp.concatenate(bwd_states, axis=1))
```

```python
import functools

import jax
import jax.numpy as jnp
from jax import lax
from jax.experimental import pallas as pl
from jax.experimental.pallas import tpu as pltpu

F32 = jnp.float32
BF16 = jnp.bfloat16
I32 = jnp.int32

D_MODEL = 1024
GRID_W = 64
MIX_GLA = 512
MIX_CONV = 512
GLA_HEADS = 4
GLA_DK = 64
GLA_DV = 128
GLA_KEY = GLA_HEADS * GLA_DK
GATE_RANK = 16
GATE_TAU = 16.0
CHUNK = 64
CONV_W = 31
CONV_HALF = CONV_W // 2
N_EXPERTS = 16
EC_FACTOR = 2
D_EXPERT = 1024
EPS = 1e-6

LANES = 128
SUBLANES = 8

PROJ_KEEP = 2 * GLA_KEY + 2 * MIX_GLA + LANES
PROJ_COLS = PROJ_KEEP + 2 * MIX_CONV
COLBLK = 512
Z_COLBLK = (2 * GLA_KEY + 2 * MIX_GLA) // LANES

VMEM_LIMIT = 56 * 1024 * 1024
CONV_PAD = 16
CONV_CH = 64
OUTPROJ_SUB = 128
GLA_GROUP = 4


def _cparams():
    return pltpu.CompilerParams(vmem_limit_bytes=VMEM_LIMIT)


def _silu(x):
    return x * jax.nn.sigmoid(x)


def _rms(x):
    return x * lax.rsqrt(jnp.mean(x * x, axis=-1, keepdims=True) + EPS)


def _dot(a, b):
    return jnp.dot(a, b, preferred_element_type=F32)


def _mod_kernel(c_ref, w_ref, b_ref, o_ref):
    s = _silu(c_ref[...]).astype(BF16)
    o_ref[0] = _dot(s, w_ref[0].astype(BF16)) + b_ref[0]


def _modulation(cond, w_mod, b_mod):
    depth = w_mod.shape[0]
    n_out = w_mod.shape[2]
    tn = 1536
    rows = cond.shape[0]
    return pl.pallas_call(
        _mod_kernel,
        grid=(depth, n_out // tn),
        in_specs=[
            pl.BlockSpec((rows, D_MODEL), lambda l, j: (0, 0)),
            pl.BlockSpec((1, D_MODEL, tn), lambda l, j: (l, 0, j)),
            pl.BlockSpec((1, 1, tn), lambda l, j: (l, 0, j)),
        ],
        out_specs=pl.BlockSpec((1, rows, tn), lambda l, j: (l, 0, j)),
        out_shape=jax.ShapeDtypeStruct((depth, rows, n_out), F32),
        compiler_params=_cparams(),
        name="modulation",
    )(cond, w_mod, b_mod.reshape(depth, 1, n_out))


def _inproj_conv_kernel(x_ref, mod_ref, g_ref, w_ref, cw_ref, cb_ref, lg_ref, lb_ref,
                        proj_ref, u_ref, pad_ref, y_ref, *, nrows, rlen):
    stride = rlen + 2 * CONV_PAD
    ntile = MIX_CONV // LANES
    tb = nrows * rlen

    @pl.when(pl.program_id(0) == 0)
    def _():
        pad_ref[...] = jnp.zeros(pad_ref.shape, F32)

    assert rlen == CONV_CH or nrows == 1
    pstep = stride if rlen == CONV_CH else CONV_CH
    off = CONV_PAD - CONV_HALF
    ngrp = CONV_CH // SUBLANES
    nshift = (CONV_CH + 2 * CONV_PAD) // SUBLANES - 1
    for c in range(tb // CONV_CH):
        for lt in range(ntile):
            win = pad_ref[lt, c * pstep:c * pstep + CONV_CH + 2 * CONV_PAD, :]
            acc = None
            for s in range(SUBLANES):
                xs = win if s == 0 else win[s:s + nshift * SUBLANES, :]
                xs = xs.reshape(-1, SUBLANES, LANES)
                for a in range(2 * CONV_PAD // SUBLANES):
                    j = SUBLANES * a + s - off
                    if 0 <= j < CONV_W:
                        term = xs[a:a + ngrp] * cw_ref[0, lt, j][None]
                        acc = term if acc is None else acc + term
            y_ref[lt, c * ngrp:(c + 1) * ngrp] = acc + cb_ref[0, lt][None]
    y = jnp.concatenate([y_ref[lt].reshape(tb, LANES) for lt in range(ntile)], axis=1)
    yc = y - jnp.mean(y, axis=-1, keepdims=True)
    yn = yc * lax.rsqrt(jnp.mean(yc * yc, axis=-1, keepdims=True) + EPS)
    u_ref[...] = _silu(yn * lg_ref[0] + lb_ref[0]).astype(BF16)

    m = mod_ref[0]
    sh = m[:, 0:D_MODEL]
    sc = m[:, D_MODEL:2 * D_MODEL]
    h = (_rms(x_ref[...]) * g_ref[0]) * (1.0 + sc) + sh
    p = _dot(h.astype(BF16), w_ref[0])
    proj_ref[...] = p[:, 0:PROJ_KEEP]
    glu = p[:, PROJ_KEEP:PROJ_KEEP + MIX_CONV] * jax.nn.sigmoid(p[:, PROJ_KEEP + MIX_CONV:PROJ_COLS])
    for lt in range(ntile):
        for r in range(nrows):
            base = r * stride + CONV_PAD
            pad_ref[lt, base:base + rlen, :] = glu[r * rlen:(r + 1) * rlen, lt * LANES:(lt + 1) * LANES]


def _inproj_conv(x, modp, norm_g, w_in_b, conv_w, conv_b, ln_g, ln_b, layer, bsz, t, rlen):
    tb = max(rlen, min(t, 512))
    nrows = tb // rlen
    nt = t // tb
    last = bsz * nt - 1
    ntile = MIX_CONV // LANES
    cur = lambda s: jnp.minimum(s, last)
    prev = lambda s: jnp.maximum(s - 1, 0)
    vec = pl.BlockSpec((1, 1, MIX_CONV), lambda s: (layer, 0, 0))
    return pl.pallas_call(
        functools.partial(_inproj_conv_kernel, nrows=nrows, rlen=rlen),
        grid=(bsz * nt + 1,),
        in_specs=[
            pl.BlockSpec((tb, D_MODEL), lambda s: (cur(s), 0)),
            pl.BlockSpec((1, 1, 6 * D_MODEL), lambda s: (cur(s) // nt, 0, 0)),
            pl.BlockSpec((1, 1, D_MODEL), lambda s: (layer, 0, 0)),
            pl.BlockSpec((1, D_MODEL, PROJ_COLS), lambda s: (layer, 0, 0)),
            pl.BlockSpec((1, ntile, CONV_W, SUBLANES, LANES), lambda s: (layer, 0, 0, 0, 0)),
            pl.BlockSpec((1, ntile, SUBLANES, LANES), lambda s: (layer, 0, 0, 0)),
            vec, vec,
        ],
        out_specs=[
            pl.BlockSpec((tb, PROJ_KEEP), lambda s: (cur(s), 0)),
            pl.BlockSpec((tb, MIX_CONV), lambda s: (prev(s), 0)),
        ],
        out_shape=[
            jax.ShapeDtypeStruct((bsz * t, PROJ_KEEP), F32),
            jax.ShapeDtypeStruct((bsz * t, MIX_CONV), BF16),
        ],
        scratch_shapes=[
            pltpu.VMEM((ntile, nrows * (rlen + 2 * CONV_PAD), LANES), F32),
            pltpu.VMEM((ntile, tb // SUBLANES, SUBLANES, LANES), F32),
        ],
        compiler_params=_cparams(),
        name="inproj_conv",
    )(x, modp, norm_g, w_in_b, conv_w, conv_b, ln_g, ln_b)


def _log_sigmoid(x):
    return jnp.minimum(x, 0.0) - jnp.log(1.0 + jnp.exp(-jnp.abs(x)))


def _head_stack(x, lane_head):
    return jnp.concatenate([jnp.where(lane_head == h, x, 0.0) for h in range(GLA_HEADS)], axis=0).astype(BF16)


def _gla_group(chains, lane_head, s_ref, rhs_ref):
    cums = []
    for qk, v, g, tri, causal, d, last_row in chains:
        g_hi = g.astype(BF16)
        g_lo = (g - g_hi.astype(F32)).astype(BF16)
        cums.append(_dot(tri, g_hi) + _dot(tri, g_lo))
    pre = []
    for (qk, v, g, tri, causal, d, last_row), cum in zip(chains, cums):
        cl = cum[last_row:last_row + 1, :]
        q = qk[:, 0:GLA_KEY]
        k = qk[:, GLA_KEY:2 * GLA_KEY]
        qd = (q * (jnp.exp(cum) * (GLA_DK ** -0.5))).astype(BF16)
        k_inv = k * jnp.exp(-cum)
        k_end = k_inv * jnp.exp(cl)
        tr = jnp.concatenate([k_end, cum], axis=0).T
        pre.append((qd, _head_stack(k_inv, lane_head), tr, v.astype(BF16)))
    mm = []
    for (qk, v, g, tri, causal, d, last_row), (qd, ks, tr, vb) in zip(chains, pre):
        sc = lax.dot_general(qd, ks, (((1,), (1,)), ((), ())), preferred_element_type=F32)
        upd = [_dot(tr[h * GLA_DK:(h + 1) * GLA_DK, 0:CHUNK].astype(BF16),
                    vb[:, h * GLA_DV:(h + 1) * GLA_DV]) for h in range(GLA_HEADS)]
        mm.append((sc, jnp.concatenate(upd, axis=0)))
    state = {}
    lhs = []
    for i, ((qk, v, g, tri, causal, d, last_row), (qd, ks, tr, vb), (sc, upd)) in enumerate(zip(chains, pre, mm)):
        s_prev = state[d] if d in state else s_ref[d]
        s_prev_b = s_prev.astype(BF16)
        for h in range(GLA_HEADS):
            rows = slice(h * GLA_DK, (h + 1) * GLA_DK)
            cols = slice(h * GLA_DV, (h + 1) * GLA_DV)
            rhs_ref[i, rows, cols] = vb[:, cols]
            rhs_ref[i, GLA_KEY + h * GLA_DK:GLA_KEY + (h + 1) * GLA_DK, cols] = s_prev_b[rows, :]
        decay = jnp.exp(tr[:, CHUNK + last_row:CHUNK + last_row + 1])
        state[d] = decay * s_prev + upd
        lhs.append(jnp.concatenate([(sc * causal).astype(BF16), qd], axis=1))
    outs = [_dot(l, rhs_ref[i]) for i, l in enumerate(lhs)]
    for d, s_new in state.items():
        s_ref[d] = s_new
    return outs


def _gla_kernel(*refs, t, zero_init, emit_state):
    qk_ref, v_ref, z_ref, wf_ref, bf_ref, wb_ref, bb_ref = refs[:7]
    pos = 7
    if not zero_init:
        s0f_ref, s0b_ref = refs[pos:pos + 2]
        pos += 2
    o_ref = refs[pos]
    pos += 1
    if emit_state:
        sf_ref, sb_ref = refs[pos:pos + 2]
        pos += 2
    gf_s, gb_s, s_ref, mask_s, rhs_s = refs[pos:pos + 5]
    rhs_s[...] = jnp.zeros(rhs_s.shape, BF16)

    z = z_ref[...].astype(BF16)
    gf_s[...] = _log_sigmoid(_dot(z, wf_ref[0]) + bf_ref[0]) * (1.0 / GATE_TAU)
    gb_s[...] = _log_sigmoid(_dot(z, wb_ref[0]) + bb_ref[0]) * (1.0 / GATE_TAU)
    if zero_init:
        s_ref[...] = jnp.zeros(s_ref.shape, F32)
    else:
        s_ref[0] = s0f_ref[0, 0].reshape(GLA_KEY, GLA_DV)
        s_ref[1] = s0b_ref[0, 0].reshape(GLA_KEY, GLA_DV)

    ri = lax.broadcasted_iota(I32, (CHUNK, CHUNK), 0)
    ci = lax.broadcasted_iota(I32, (CHUNK, CHUNK), 1)
    tri_low = jnp.where(ri >= ci, 1.0, 0.0).astype(BF16)
    tri_upp = jnp.where(ri <= ci, 1.0, 0.0).astype(BF16)
    rs = lax.broadcasted_iota(I32, (CHUNK, GLA_HEADS * CHUNK), 0)
    cs = lax.broadcasted_iota(I32, (CHUNK, GLA_HEADS * CHUNK), 1) % CHUNK
    mask_s[0] = jnp.where(rs >= cs, 1.0, 0.0)
    mask_s[1] = jnp.where(rs <= cs, 1.0, 0.0)
    lane_head = lax.broadcasted_iota(I32, (CHUNK, GLA_KEY), 1) // GLA_DK
    n = t // CHUNK

    group = min(GLA_GROUP, n // 2)

    def step(i, first):
        rows = []
        chains = []
        for j in range(group):
            c = i * group + j
            rf = pl.multiple_of(c * CHUNK, CHUNK)
            rb = pl.multiple_of((n - 1 - c) * CHUNK, CHUNK)
            rows += [rf, rb]
            chains.append((qk_ref[pl.ds(rf, CHUNK), :], v_ref[pl.ds(rf, CHUNK), :],
                           gf_s[pl.ds(rf, CHUNK), :], tri_low, mask_s[0], 0, CHUNK - 1))
            chains.append((qk_ref[pl.ds(rb, CHUNK), :], v_ref[pl.ds(rb, CHUNK), :],
                           gb_s[pl.ds(rb, CHUNK), :], tri_upp, mask_s[1], 1, 0))
        outs = _gla_group(chains, lane_head, s_ref, rhs_s)
        for r, o in zip(rows, outs):
            if first:
                o_ref[pl.ds(r, CHUNK), :] = o
            else:
                o_ref[pl.ds(r, CHUNK), :] += o

    def first_half(i, c):
        step(i, True)
        return c

    def second_half(i, c):
        step(i, False)
        return c

    half = n // (2 * group)
    lax.fori_loop(0, half, first_half, 0)
    lax.fori_loop(half, 2 * half, second_half, 0)
    if emit_state:
        sf_ref[0, 0] = s_ref[0].reshape(GLA_HEADS, GLA_DK, GLA_DV)
        sb_ref[0, 0] = s_ref[1].reshape(GLA_HEADS, GLA_DK, GLA_DV)


def _gla(proj, wdf, bdf, wdb, bdb, s0f, s0b, layer, bsz, t, emit_state):
    zero_init = s0f is None
    st_block = (1, 1, GLA_HEADS, GLA_DK, GLA_DV)
    in_specs = [
        pl.BlockSpec((t, COLBLK), lambda b: (b, 0)),
        pl.BlockSpec((t, COLBLK), lambda b: (b, 1)),
        pl.BlockSpec((t, LANES), lambda b: (b, Z_COLBLK)),
        pl.BlockSpec((1, LANES, GLA_KEY), lambda b: (layer, 0, 0)),
        pl.BlockSpec((1, 1, GLA_KEY), lambda b: (layer, 0, 0)),
        pl.BlockSpec((1, LANES, GLA_KEY), lambda b: (layer, 0, 0)),
        pl.BlockSpec((1, 1, GLA_KEY), lambda b: (layer, 0, 0)),
    ]
    args = [proj, proj, proj, wdf, bdf, wdb, bdb]
    if not zero_init:
        in_specs += [pl.BlockSpec(st_block, lambda b: (b, layer, 0, 0, 0))] * 2
        args += [s0f, s0b]
    out_specs = [pl.BlockSpec((t, MIX_GLA), lambda b: (b, 0))]
    out_shape = [jax.ShapeDtypeStruct((bsz * t, MIX_GLA), F32)]
    if emit_state:
        out_specs += [pl.BlockSpec((1, 1, GLA_HEADS, GLA_DK, GLA_DV), lambda b: (b, 0, 0, 0, 0))] * 2
        out_shape += [jax.ShapeDtypeStruct((bsz, 1, GLA_HEADS, GLA_DK, GLA_DV), F32)] * 2
    return pl.pallas_call(
        functools.partial(_gla_kernel, t=t, zero_init=zero_init, emit_state=emit_state),
        grid=(bsz,),
        in_specs=in_specs,
        out_specs=out_specs,
        out_shape=out_shape,
        scratch_shapes=[
            pltpu.VMEM((t, GLA_KEY), F32),
            pltpu.VMEM((t, GLA_KEY), F32),
            pltpu.VMEM((2, GLA_KEY, GLA_DV), F32),
            pltpu.VMEM((2, CHUNK, GLA_HEADS * CHUNK), F32),
            pltpu.VMEM((2 * min(GLA_GROUP, t // CHUNK // 2), 2 * GLA_KEY, MIX_GLA), BF16),
        ],
        compiler_params=_cparams(),
        name="gla",
    )(*args)


def _outproj_kernel(o_ref, r_ref, u_ref, x_ref, mod_ref, gn_ref, wo_ref, g2_ref, wr_ref,
                    xo_ref, h2_ref, aff_ref):
    gn = gn_ref[0]
    m = mod_ref[0]
    gt1 = m[:, 2 * D_MODEL:3 * D_MODEL]
    sh2 = m[:, 3 * D_MODEL:4 * D_MODEL]
    sc2 = m[:, 4 * D_MODEL:5 * D_MODEL]
    tm = o_ref.shape[0]
    subs = [slice(r0, r0 + OUTPROJ_SUB) for r0 in range(0, tm, OUTPROJ_SUB)]
    acts = []
    for rs in subs:
        o = o_ref[rs, :]
        r = r_ref[rs, :]
        parts = []
        for h in range(GLA_HEADS):
            hs = slice(h * GLA_DV, (h + 1) * GLA_DV)
            parts.append(((_rms(o[:, hs]) * gn) * _silu(r[:, hs])).astype(BF16))
        parts.append(u_ref[rs, :])
        acts.append(jnp.concatenate(parts, axis=1))
    mixed = [_dot(a, wo_ref[0]) for a in acts]
    h2s = []
    for rs, mix in zip(subs, mixed):
        x = x_ref[rs, :] + gt1 * mix
        xo_ref[rs, :] = x
        h2 = ((_rms(x) * g2_ref[0]) * (1.0 + sc2) + sh2).astype(BF16)
        h2_ref[rs, :] = h2
        h2s.append(h2)
    logit = [_dot(h2, wr_ref[0]) for h2 in h2s]
    for rs, logits in zip(subs, logit):
        lane = lax.broadcasted_iota(I32, logits.shape, 1)
        logits = jnp.where(lane < N_EXPERTS, logits, -jnp.inf)
        ex = jnp.exp(logits - jnp.max(logits, axis=-1, keepdims=True))
        aff_ref[rs, :] = ex / jnp.sum(ex, axis=-1, keepdims=True)


def _outproj(o, proj, u, x, modp, gla_norm_g, w_out_b, norm_ffn_g, w_router_b, layer, bsz, t):
    tm = min(t, 512)
    nt = t // tm
    row = lambda b, i: (b * nt + i, 0)
    return pl.pallas_call(
        _outproj_kernel,
        grid=(bsz, nt),
        in_specs=[
            pl.BlockSpec((tm, MIX_GLA), row),
            pl.BlockSpec((tm, COLBLK), lambda b, i: (b * nt + i, 2)),
            pl.BlockSpec((tm, MIX_CONV), row),
            pl.BlockSpec((tm, D_MODEL), row),
            pl.BlockSpec((1, 1, 6 * D_MODEL), lambda b, i: (b, 0, 0)),
            pl.BlockSpec((1, 1, GLA_DV), lambda b, i: (layer, 0, 0)),
            pl.BlockSpec((1, D_MODEL, D_MODEL), lambda b, i: (layer, 0, 0)),
            pl.BlockSpec((1, 1, D_MODEL), lambda b, i: (layer, 0, 0)),
            pl.BlockSpec((1, D_MODEL, LANES), lambda b, i: (layer, 0, 0)),
        ],
        out_specs=[
            pl.BlockSpec((tm, D_MODEL), row),
            pl.BlockSpec((tm, D_MODEL), row),
            pl.BlockSpec((tm, LANES), row),
        ],
        out_shape=[
            jax.ShapeDtypeStruct((bsz * t, D_MODEL), F32),
            jax.ShapeDtypeStruct((bsz * t, D_MODEL), BF16),
            jax.ShapeDtypeStruct((bsz * t, LANES), F32),
        ],
        compiler_params=_cparams(),
        name="outproj",
    )(o, proj, u, x, modp, gla_norm_g, w_out_b, norm_ffn_g, w_router_b)


PREFIX_BLK = 256


def _excl_prefix(x, t):
    blk = min(t, PREFIX_BLK)
    ri = lax.broadcasted_iota(I32, (blk, blk), 0)
    ci = lax.broadcasted_iota(I32, (blk, blk), 1)
    upper = jnp.where(ri < ci, 1.0, 0.0).astype(BF16)
    run = jnp.zeros((x.shape[0], 1), F32)
    outs = []
    for j in range(t // blk):
        xb = x[:, j * blk:(j + 1) * blk]
        outs.append(_dot(xb.astype(BF16), upper) + run)
        run = run + jnp.sum(xb, axis=1, keepdims=True)
    return jnp.concatenate(outs, axis=1) if len(outs) > 1 else outs[0]


F32_MAGNITUDE_BITS = 31
REFINE_STEPS = 16


def _route_kernel(aff_ref, sel_ref, selt_ref, afft_ref, at_s, *, bsz, t, cap):
    for b in range(bsz):
        at_s[b * N_EXPERTS:(b + 1) * N_EXPERTS, :] = aff_ref[b * t:(b + 1) * t, :].T[0:N_EXPERTS, :]
    a = at_s[...]
    afft_ref[...] = a
    rows = bsz * N_EXPERTS
    capf = float(cap)

    def enough(th):
        return jnp.sum(jnp.where(a >= th, 1.0, 0.0), axis=1, keepdims=True) >= capf

    def bit_step(i, thr):
        cand = thr | jnp.left_shift(jnp.int32(1), F32_MAGNITUDE_BITS - 1 - i)
        return jnp.where(enough(lax.bitcast_convert_type(cand, F32)), cand, thr)

    thr = lax.fori_loop(0, F32_MAGNITUDE_BITS, bit_step, jnp.zeros((rows, 1), I32))
    lo = lax.bitcast_convert_type(thr, F32)
    hi = lax.bitcast_convert_type(thr + 1, F32)

    def refine(i, c):
        lo, hi = c
        mid = lo + (hi - lo) * 0.5
        ok = enough(mid)
        return jnp.where(ok, mid, lo), jnp.where(ok, hi, mid)

    lo, hi = lax.fori_loop(0, REFINE_STEPS, refine, (lo, hi))
    gt = jnp.where(a >= hi, 1.0, 0.0) * jnp.where(a > lo, 1.0, 0.0)
    eq = jnp.where(a >= lo, 1.0, 0.0) - gt
    need = capf - jnp.sum(gt, axis=1, keepdims=True)
    tie_rank = _excl_prefix(eq, t)
    m = gt + eq * jnp.where(tie_rank < need, 1.0, 0.0)
    slot = _excl_prefix(m, t)
    sel = jnp.where(m > 0.5, slot.astype(I32), -1)
    sel_ref[...] = sel
    filler = jnp.full((LANES - N_EXPERTS, t), -1, I32)
    for b in range(bsz):
        selt_ref[b * t:(b + 1) * t, :] = jnp.concatenate(
            [sel[b * N_EXPERTS:(b + 1) * N_EXPERTS, :], filler], axis=0).T


def _route(aff, bsz, t, cap):
    rows = bsz * N_EXPERTS
    sel, selt, afft = pl.pallas_call(
        functools.partial(_route_kernel, bsz=bsz, t=t, cap=cap),
        grid=(1,),
        in_specs=[pl.BlockSpec((bsz * t, LANES), lambda i: (0, 0))],
        out_specs=[
            pl.BlockSpec((rows, t), lambda i: (0, 0)),
            pl.BlockSpec((bsz * t, LANES), lambda i: (0, 0)),
            pl.BlockSpec((rows, t), lambda i: (0, 0)),
        ],
        out_shape=[
            jax.ShapeDtypeStruct((rows, t), I32),
            jax.ShapeDtypeStruct((bsz * t, LANES), I32),
            jax.ShapeDtypeStruct((rows, t), F32),
        ],
        scratch_shapes=[pltpu.VMEM((rows, t), F32)],
        compiler_params=_cparams(),
        name="route",
    )(aff)
    return sel.reshape(bsz, N_EXPERTS, t), selt, afft.reshape(bsz, N_EXPERTS, t)


def _gather_kernel(h_ref, sel_ref, afft_ref, xs_ref, gate_ref, *, t, cap, eg):
    g = pl.program_id(1)
    hb = h_ref[...]
    slot = lax.broadcasted_iota(I32, (cap, t), 0)
    for el in range(eg):
        if eg == N_EXPERTS:
            srow = sel_ref[0, el:el + 1, :]
            arow = afft_ref[0, el:el + 1, :]
        else:
            e = g * eg + el
            srow = sel_ref[0, pl.ds(e, 1), :]
            arow = afft_ref[0, pl.ds(e, 1), :]
        hit = srow == slot
        p = jnp.where(hit, 1.0, 0.0).astype(BF16)
        xs_ref[el, 0] = _dot(p, hb).astype(BF16)
        gate_ref[el, 0] = jnp.sum(jnp.where(hit, arow, 0.0), axis=1, keepdims=True)


def _gather(h2, sel, afft, bsz, t, cap):
    eg = max(1, min(N_EXPERTS, 512 // cap))
    return pl.pallas_call(
        functools.partial(_gather_kernel, t=t, cap=cap, eg=eg),
        grid=(bsz, N_EXPERTS // eg),
        in_specs=[
            pl.BlockSpec((t, D_MODEL), lambda b, g: (b, 0)),
            pl.BlockSpec((1, N_EXPERTS, t), lambda b, g: (b, 0, 0)),
            pl.BlockSpec((1, N_EXPERTS, t), lambda b, g: (b, 0, 0)),
        ],
        out_specs=[
            pl.BlockSpec((eg, 1, cap, D_MODEL), lambda b, g: (g, b, 0, 0)),
            pl.BlockSpec((eg, 1, cap, 1), lambda b, g: (g, b, 0, 0)),
        ],
        out_shape=[
            jax.ShapeDtypeStruct((N_EXPERTS, bsz, cap, D_MODEL), BF16),
            jax.ShapeDtypeStruct((N_EXPERTS, bsz, cap, 1), F32),
        ],
        compiler_params=_cparams(),
        name="gather",
    )(h2, sel, afft)


FFN_TF = 256
FFN_TM = 512


def _ffn_kernel(xc_ref, xl_ref, gc_ref, gl_ref, wg_ref, wu_ref, wd_ref, yc_ref, yl_ref, acc_ref):
    def step(first):
        wg = wg_ref[0, 0].astype(BF16)
        wu = wu_ref[0, 0].astype(BF16)
        wd = wd_ref[0, 0].astype(BF16)
        base = 0
        for x_ref, g_ref, y_ref in ((xc_ref, gc_ref, yc_ref), (xl_ref, gl_ref, yl_ref)):
            rows = x_ref.shape[1]
            for r0 in range(0, rows, FFN_TM):
                x = x_ref[0, r0:r0 + FFN_TM, :]
                hm = (_silu(_dot(x, wg)) * _dot(x, wu)).astype(BF16)
                part = _dot(hm, wd)
                if first:
                    acc_ref[base + r0:base + r0 + FFN_TM, :] = part
                else:
                    acc = acc_ref[base + r0:base + r0 + FFN_TM, :] + part
                    acc_ref[base + r0:base + r0 + FFN_TM, :] = acc
                    y_ref[0, r0:r0 + FFN_TM, :] = (acc * g_ref[0, r0:r0 + FFN_TM, :]).astype(BF16)
            base += rows

    j = pl.program_id(1)
    pl.when(j == 0)(functools.partial(step, True))
    pl.when(j > 0)(functools.partial(step, False))


def _ffn(xs_c, gates_c, xs_l, gates_l, w_e_gate, w_e_up, w_e_down, layer):
    rc = xs_c.shape[1]
    rl = xs_l.shape[1]
    assert rc % FFN_TM == 0 and rl % FFN_TM == 0
    rowblk = lambda r, w: pl.BlockSpec((1, r, w), lambda e, j: (e, 0, 0))
    return pl.pallas_call(
        _ffn_kernel,
        grid=(N_EXPERTS, D_EXPERT // FFN_TF),
        in_specs=[
            rowblk(rc, D_MODEL), rowblk(rl, D_MODEL), rowblk(rc, 1), rowblk(rl, 1),
            pl.BlockSpec((1, 1, D_MODEL, FFN_TF), lambda e, j: (layer, e, 0, j)),
            pl.BlockSpec((1, 1, D_MODEL, FFN_TF), lambda e, j: (layer, e, 0, j)),
            pl.BlockSpec((1, 1, FFN_TF, D_MODEL), lambda e, j: (layer, e, j, 0)),
        ],
        out_specs=[rowblk(rc, D_MODEL), rowblk(rl, D_MODEL)],
        out_shape=[
            jax.ShapeDtypeStruct((N_EXPERTS, rc, D_MODEL), BF16),
            jax.ShapeDtypeStruct((N_EXPERTS, rl, D_MODEL), BF16),
        ],
        scratch_shapes=[pltpu.VMEM((rc + rl, D_MODEL), F32)],
        compiler_params=_cparams(),
        name="ffn",
    )(xs_c, xs_l, gates_c, gates_l, w_e_gate, w_e_up, w_e_down)


def _scatter_kernel(*refs, cap, final):
    if final:
        selt_ref, y_ref, x_ref, mod_ref, gf_ref, xo_ref, pt_s = refs
    else:
        selt_ref, y_ref, x_ref, mod_ref, xo_ref, pt_s = refs
    tt = selt_ref.shape[0]
    ncol = N_EXPERTS * cap
    gw = min(ncol, 1024)
    if cap % LANES == 0:
        st = selt_ref[...]
        slot = lax.broadcasted_iota(I32, (1, cap), 1)
        for e in range(N_EXPERTS):
            pt_s[:, e * cap:(e + 1) * cap] = jnp.where(st[:, e:e + 1] == slot, 1.0, 0.0).astype(BF16)
    else:
        stf = selt_ref[...].astype(F32).astype(BF16)
        col = lax.broadcasted_iota(I32, (LANES, gw), 1)
        row = lax.broadcasted_iota(I32, (LANES, gw), 0)
        cmod = (lax.broadcasted_iota(I32, (1, gw), 1) % cap).astype(F32)
        for g in range(ncol // gw):
            expand = jnp.where((col + g * gw) // cap == row, 1.0, 0.0).astype(BF16)
            selx = _dot(stf, expand)
            pt_s[:, g * gw:(g + 1) * gw] = jnp.where(selx == cmod, 1.0, 0.0).astype(BF16)
    yb = y_ref[:, 0].reshape(ncol, D_MODEL)
    gt2 = mod_ref[0][:, 5 * D_MODEL:6 * D_MODEL]
    x = x_ref[...] + gt2 * _dot(pt_s[...], yb)
    if final:
        x = _rms(x) * gf_ref[...]
    xo_ref[...] = x


def _scatter(selt, y, x, modp, norm_final_g, bsz, t, cap):
    final = norm_final_g is not None
    tt = min(t, 512)
    nt = t // tt
    row = lambda b, i: (b * nt + i, 0)
    in_specs = [
        pl.BlockSpec((tt, LANES), row),
        pl.BlockSpec((N_EXPERTS, 1, cap, D_MODEL), lambda b, i: (0, b, 0, 0)),
        pl.BlockSpec((tt, D_MODEL), row),
        pl.BlockSpec((1, 1, 6 * D_MODEL), lambda b, i: (b, 0, 0)),
    ]
    args = [selt, y, x, modp]
    if final:
        in_specs.append(pl.BlockSpec((1, D_MODEL), lambda b, i: (0, 0)))
        args.append(norm_final_g)
    return pl.pallas_call(
        functools.partial(_scatter_kernel, cap=cap, final=final),
        grid=(bsz, nt),
        in_specs=in_specs,
        out_specs=pl.BlockSpec((tt, D_MODEL), row),
        out_shape=jax.ShapeDtypeStruct((bsz * t, D_MODEL), F32),
        scratch_shapes=[pltpu.VMEM((tt, N_EXPERTS * cap), BF16)],
        compiler_params=_cparams(),
        name="scatter",
    )(*args)


def _mixer_and_route(x, modp, s0f, s0b, wts, layer, bsz, t, rlen, emit_state):
    cap = EC_FACTOR * t // N_EXPERTS
    proj, u = _inproj_conv(x, modp, wts["norm_mix_g"], wts["w_in"], wts["conv_w"], wts["conv_b"],
                           wts["conv_ln_g"], wts["conv_ln_b"], layer, bsz, t, rlen)
    gla = _gla(proj, wts["wdf"], wts["bdf"], wts["wdb"], wts["bdb"], s0f, s0b, layer, bsz, t, emit_state)
    if emit_state:
        o, sf, sb = gla
    else:
        (o,) = gla
        sf = sb = None
    x, h2, aff = _outproj(o, proj, u, x, modp, wts["gla_norm_g"], wts["w_out"], wts["norm_ffn_g"],
                          wts["w_router"], layer, bsz, t)
    sel, selt, afft = _route(aff, bsz, t, cap)
    xs, gates = _gather(h2, sel, afft, bsz, t, cap)
    return (x, selt, xs.reshape(N_EXPERTS, bsz * cap, D_MODEL), gates.reshape(N_EXPERTS, bsz * cap, 1)), sf, sb


def kernel(x_prompt, x_sample, state_gla_fwd, state_gla_bwd, c, c_ctx, norm_mix_g, norm_ffn_g, norm_final_g, w_mod, b_mod, w_in, w_decay_f, b_decay_f, w_decay_b, b_decay_b, gla_norm_g, conv_w, conv_b, conv_ln_g, conv_ln_b, w_out, w_router, w_e_gate, w_e_up, w_e_down):
    depth = w_in.shape[0]
    bp, tp, _ = x_prompt.shape
    bs, ts, _ = x_sample.shape

    zcols = 2 * GLA_KEY + 2 * MIX_GLA
    w_in_r = jnp.concatenate(
        [w_in[:, :, :zcols + 2 * GATE_RANK], jnp.zeros((depth, D_MODEL, LANES - 2 * GATE_RANK), F32),
         w_in[:, :, zcols + 2 * GATE_RANK:]], axis=2).astype(BF16)
    zpad_f = jnp.zeros((depth, LANES - GATE_RANK, GLA_KEY), F32)
    zpad_b0 = jnp.zeros((depth, GATE_RANK, GLA_KEY), F32)
    zpad_b1 = jnp.zeros((depth, LANES - 2 * GATE_RANK, GLA_KEY), F32)
    vec = lambda a: a.reshape(depth, 1, a.shape[-1])
    wts = dict(
        norm_mix_g=vec(norm_mix_g), norm_ffn_g=vec(norm_ffn_g), w_in=w_in_r,
        wdf=jnp.concatenate([w_decay_f, zpad_f], axis=1).astype(BF16), bdf=vec(b_decay_f),
        wdb=jnp.concatenate([zpad_b0, w_decay_b, zpad_b1], axis=1).astype(BF16), bdb=vec(b_decay_b),
        gla_norm_g=vec(gla_norm_g),
        conv_w=jnp.broadcast_to(
            conv_w.reshape(depth, CONV_W, 1, MIX_CONV // LANES, LANES).transpose(0, 3, 1, 2, 4),
            (depth, MIX_CONV // LANES, CONV_W, SUBLANES, LANES)),
        conv_b=jnp.broadcast_to(conv_b.reshape(depth, MIX_CONV // LANES, 1, LANES),
                                (depth, MIX_CONV // LANES, SUBLANES, LANES)),
        conv_ln_g=vec(conv_ln_g), conv_ln_b=vec(conv_ln_b), w_out=w_out.astype(BF16),
        w_router=jnp.concatenate(
            [w_router, jnp.zeros((depth, D_MODEL, LANES - N_EXPERTS), F32)], axis=2).astype(BF16),
        w_e_gate=w_e_gate, w_e_up=w_e_up, w_e_down=w_e_down,
    )

    nrow = 2 * SUBLANES
    cond = jnp.concatenate([c, c_ctx[None, :], jnp.zeros((nrow - bs - 1, D_MODEL), F32)], axis=0)
    mod = _modulation(cond, w_mod, b_mod)

    xp = x_prompt.reshape(bp * tp, D_MODEL)
    xs = x_sample.reshape(bs * ts, D_MODEL)
    final_g = norm_final_g.reshape(1, D_MODEL)
    fwd_states = []
    bwd_states = []
    for l in range(depth):
        last = l == depth - 1
        mod_ctx = jnp.broadcast_to(mod[l, bs][None, None, :], (bp, 1, 6 * D_MODEL))
        mod_lat = mod[l, :bs][:, None, :]
        (xp, selt_c, xs_c, gates_c), sf, sb = _mixer_and_route(
            xp, mod_ctx, None, None, wts, l, bp, tp, tp, True)
        fwd_states.append(sf)
        bwd_states.append(sb)
        (xs, selt_l, xs_l, gates_l), _, _ = _mixer_and_route(
            xs, mod_lat, state_gla_fwd, state_gla_bwd, wts, l, bs, ts, GRID_W, False)
        y_c, y_l = _ffn(xs_c, gates_c, xs_l, gates_l, wts["w_e_gate"], wts["w_e_up"], wts["w_e_down"], l)
        cap_c = EC_FACTOR * tp // N_EXPERTS
        cap_l = EC_FACTOR * ts // N_EXPERTS
        fin = final_g if last else None
        xp = _scatter(selt_c, y_c.reshape(N_EXPERTS, bp, cap_c, D_MODEL), xp, mod_ctx, fin, bp, tp, cap_c)
        xs = _scatter(selt_l, y_l.reshape(N_EXPERTS, bs, cap_l, D_MODEL), xs, mod_lat, fin, bs, ts, cap_l)
    y_prompt = xp.reshape(bp, tp, D_MODEL)
    y_sample = xs.reshape(bs, ts, D_MODEL)
    return (y_prompt, y_sample, jnp.concatenate(fwd_states, axis=1), jnp.concatenate(bwd_states, axis=1))
```

```python
import functools

import jax
import jax.numpy as jnp
from jax import lax
from jax.experimental import pallas as pl
from jax.experimental.pallas import tpu as pltpu

F32 = jnp.float32
BF16 = jnp.bfloat16
I32 = jnp.int32

D_MODEL = 1024
GRID_W = 64
MIX_GLA = 512
MIX_CONV = 512
GLA_HEADS = 4
GLA_DK = 64
GLA_DV = 128
GLA_KEY = GLA_HEADS * GLA_DK
GATE_RANK = 16
GATE_TAU = 16.0
CHUNK = 64
CONV_W = 31
CONV_HALF = CONV_W // 2
N_EXPERTS = 16
EC_FACTOR = 2
D_EXPERT = 1024
EPS = 1e-6

LANES = 128
SUBLANES = 8

PROJ_KEEP = 2 * GLA_KEY + 2 * MIX_GLA + LANES
PROJ_COLS = PROJ_KEEP + 2 * MIX_CONV
COLBLK = 512
Z_COLBLK = (2 * GLA_KEY + 2 * MIX_GLA) // LANES

VMEM_LIMIT = 56 * 1024 * 1024
CONV_PAD = 16
CONV_CH = 64
OUTPROJ_SUB = 128
GLA_GROUP = 4


def _cparams():
    return pltpu.CompilerParams(vmem_limit_bytes=VMEM_LIMIT)


def _silu(x):
    return x * jax.nn.sigmoid(x)


def _rms(x):
    return x * lax.rsqrt(jnp.mean(x * x, axis=-1, keepdims=True) + EPS)


def _dot(a, b):
    return jnp.dot(a, b, preferred_element_type=F32)


def _mod_kernel(c_ref, w_ref, b_ref, o_ref):
    s = _silu(c_ref[...]).astype(BF16)
    o_ref[0] = _dot(s, w_ref[0].astype(BF16)) + b_ref[0]


def _modulation(cond, w_mod, b_mod):
    depth = w_mod.shape[0]
    n_out = w_mod.shape[2]
    tn = 1536
    rows = cond.shape[0]
    return pl.pallas_call(
        _mod_kernel,
        grid=(depth, n_out // tn),
        in_specs=[
            pl.BlockSpec((rows, D_MODEL), lambda l, j: (0, 0)),
            pl.BlockSpec((1, D_MODEL, tn), lambda l, j: (l, 0, j)),
            pl.BlockSpec((1, 1, tn), lambda l, j: (l, 0, j)),
        ],
        out_specs=pl.BlockSpec((1, rows, tn), lambda l, j: (l, 0, j)),
        out_shape=jax.ShapeDtypeStruct((depth, rows, n_out), F32),
        compiler_params=_cparams(),
        name="modulation",
    )(cond, w_mod, b_mod.reshape(depth, 1, n_out))


def _inproj_conv_kernel(x_ref, mod_ref, g_ref, w_ref, cw_ref, cb_ref, lg_ref, lb_ref,
                        proj_ref, u_ref, pad_ref, y_ref, *, nrows, rlen):
    stride = rlen + 2 * CONV_PAD
    ntile = MIX_CONV // LANES
    tb = nrows * rlen

    @pl.when(pl.program_id(0) == 0)
    def _():
        pad_ref[...] = jnp.zeros(pad_ref.shape, F32)

    assert rlen == CONV_CH or nrows == 1
    pstep = stride if rlen == CONV_CH else CONV_CH
    off = CONV_PAD - CONV_HALF
    ngrp = CONV_CH // SUBLANES
    nshift = (CONV_CH + 2 * CONV_PAD) // SUBLANES - 1
    for c in range(tb // CONV_CH):
        for lt in range(ntile):
            win = pad_ref[lt, c * pstep:c * pstep + CONV_CH + 2 * CONV_PAD, :]
            acc = None
            for s in range(SUBLANES):
                xs = win if s == 0 else win[s:s + nshift * SUBLANES, :]
                xs = xs.reshape(-1, SUBLANES, LANES)
                for a in range(2 * CONV_PAD // SUBLANES):
                    j = SUBLANES * a + s - off
                    if 0 <= j < CONV_W:
                        term = xs[a:a + ngrp] * cw_ref[0, lt, j][None]
                        acc = term if acc is None else acc + term
            y_ref[lt, c * ngrp:(c + 1) * ngrp] = acc + cb_ref[0, lt][None]
    y = jnp.concatenate([y_ref[lt].reshape(tb, LANES) for lt in range(ntile)], axis=1)
    yc = y - jnp.mean(y, axis=-1, keepdims=True)
    yn = yc * lax.rsqrt(jnp.mean(yc * yc, axis=-1, keepdims=True) + EPS)
    u_ref[...] = _silu(yn * lg_ref[0] + lb_ref[0]).astype(BF16)

    m = mod_ref[0]
    sh = m[:, 0:D_MODEL]
    sc = m[:, D_MODEL:2 * D_MODEL]
    h = (_rms(x_ref[...]) * g_ref[0]) * (1.0 + sc) + sh
    p = _dot(h.astype(BF16), w_ref[0])
    proj_ref[...] = p[:, 0:PROJ_KEEP]
    glu = p[:, PROJ_KEEP:PROJ_KEEP + MIX_CONV] * jax.nn.sigmoid(p[:, PROJ_KEEP + MIX_CONV:PROJ_COLS])
    for lt in range(ntile):
        for r in range(nrows):
            base = r * stride + CONV_PAD
            pad_ref[lt, base:base + rlen, :] = glu[r * rlen:(r + 1) * rlen, lt * LANES:(lt + 1) * LANES]


def _inproj_conv(x, modp, norm_g, w_in_b, conv_w, conv_b, ln_g, ln_b, layer, bsz, t, rlen):
    tb = max(rlen, min(t, 512))
    nrows = tb // rlen
    nt = t // tb
    last = bsz * nt - 1
    ntile = MIX_CONV // LANES
    cur = lambda s: jnp.minimum(s, last)
    prev = lambda s: jnp.maximum(s - 1, 0)
    vec = pl.BlockSpec((1, 1, MIX_CONV), lambda s: (layer, 0, 0))
    return pl.pallas_call(
        functools.partial(_inproj_conv_kernel, nrows=nrows, rlen=rlen),
        grid=(bsz * nt + 1,),
        in_specs=[
            pl.BlockSpec((tb, D_MODEL), lambda s: (cur(s), 0)),
            pl.BlockSpec((1, 1, 6 * D_MODEL), lambda s: (cur(s) // nt, 0, 0)),
            pl.BlockSpec((1, 1, D_MODEL), lambda s: (layer, 0, 0)),
            pl.BlockSpec((1, D_MODEL, PROJ_COLS), lambda s: (layer, 0, 0)),
            pl.BlockSpec((1, ntile, CONV_W, SUBLANES, LANES), lambda s: (layer, 0, 0, 0, 0)),
            pl.BlockSpec((1, ntile, SUBLANES, LANES), lambda s: (layer, 0, 0, 0)),
            vec, vec,
        ],
        out_specs=[
            pl.BlockSpec((tb, PROJ_KEEP), lambda s: (cur(s), 0)),
            pl.BlockSpec((tb, MIX_CONV), lambda s: (prev(s), 0)),
        ],
        out_shape=[
            jax.ShapeDtypeStruct((bsz * t, PROJ_KEEP), F32),
            jax.ShapeDtypeStruct((bsz * t, MIX_CONV), BF16),
        ],
        scratch_shapes=[
            pltpu.VMEM((ntile, nrows * (rlen + 2 * CONV_PAD), LANES), F32),
            pltpu.VMEM((ntile, tb // SUBLANES, SUBLANES, LANES), F32),
        ],
        compiler_params=_cparams(),
        name="inproj_conv",
    )(x, modp, norm_g, w_in_b, conv_w, conv_b, ln_g, ln_b)


def _log_sigmoid(x):
    return jnp.minimum(x, 0.0) - jnp.log(1.0 + jnp.exp(-jnp.abs(x)))


def _head_stack(x, lane_head):
    return jnp.concatenate([jnp.where(lane_head == h, x, 0.0) for h in range(GLA_HEADS)], axis=0).astype(BF16)


def _gla_group(chains, lane_head, s_ref, rhs_ref):
    cums = []
    for qk, v, g, tri, causal, d, last_row in chains:
        g_hi = g.astype(BF16)
        g_lo = (g - g_hi.astype(F32)).astype(BF16)
        cums.append(_dot(tri, g_hi) + _dot(tri, g_lo))
    pre = []
    for (qk, v, g, tri, causal, d, last_row), cum in zip(chains, cums):
        cl = cum[last_row:last_row + 1, :]
        q = qk[:, 0:GLA_KEY]
        k = qk[:, GLA_KEY:2 * GLA_KEY]
        qd = (q * (jnp.exp(cum) * (GLA_DK ** -0.5))).astype(BF16)
        k_inv = k * jnp.exp(-cum)
        k_end = k_inv * jnp.exp(cl)
        tr = jnp.concatenate([k_end, cum], axis=0).T
        pre.append((qd, _head_stack(k_inv, lane_head), tr, v.astype(BF16)))
    mm = []
    for (qk, v, g, tri, causal, d, last_row), (qd, ks, tr, vb) in zip(chains, pre):
        sc = lax.dot_general(qd, ks, (((1,), (1,)), ((), ())), preferred_element_type=F32)
        upd = [_dot(tr[h * GLA_DK:(h + 1) * GLA_DK, 0:CHUNK].astype(BF16),
                    vb[:, h * GLA_DV:(h + 1) * GLA_DV]) for h in range(GLA_HEADS)]
        mm.append((sc, jnp.concatenate(upd, axis=0)))
    state = {}
    lhs = []
    for i, ((qk, v, g, tri, causal, d, last_row), (qd, ks, tr, vb), (sc, upd)) in enumerate(zip(chains, pre, mm)):
        s_prev = state[d] if d in state else s_ref[d]
        s_prev_b = s_prev.astype(BF16)
        for h in range(GLA_HEADS):
            rows = slice(h * GLA_DK, (h + 1) * GLA_DK)
            cols = slice(h * GLA_DV, (h + 1) * GLA_DV)
            rhs_ref[i, rows, cols] = vb[:, cols]
            rhs_ref[i, GLA_KEY + h * GLA_DK:GLA_KEY + (h + 1) * GLA_DK, cols] = s_prev_b[rows, :]
        decay = jnp.exp(tr[:, CHUNK + last_row:CHUNK + last_row + 1])
        state[d] = decay * s_prev + upd
        lhs.append(jnp.concatenate([(sc * causal).astype(BF16), qd], axis=1))
    outs = [_dot(l, rhs_ref[i]) for i, l in enumerate(lhs)]
    for d, s_new in state.items():
        s_ref[d] = s_new
    return outs


def _gla_kernel(*refs, t, zero_init, emit_state):
    qk_ref, v_ref, z_ref, wf_ref, bf_ref, wb_ref, bb_ref = refs[:7]
    pos = 7
    if not zero_init:
        s0f_ref, s0b_ref = refs[pos:pos + 2]
        pos += 2
    o_ref = refs[pos]
    pos += 1
    if emit_state:
        sf_ref, sb_ref = refs[pos:pos + 2]
        pos += 2
    gf_s, gb_s, s_ref, mask_s, rhs_s = refs[pos:pos + 5]
    rhs_s[...] = jnp.zeros(rhs_s.shape, BF16)

    z = z_ref[...].astype(BF16)
    gf_s[...] = _log_sigmoid(_dot(z, wf_ref[0]) + bf_ref[0]) * (1.0 / GATE_TAU)
    gb_s[...] = _log_sigmoid(_dot(z, wb_ref[0]) + bb_ref[0]) * (1.0 / GATE_TAU)
    if zero_init:
        s_ref[...] = jnp.zeros(s_ref.shape, F32)
    else:
        s_ref[0] = s0f_ref[0, 0].reshape(GLA_KEY, GLA_DV)
        s_ref[1] = s0b_ref[0, 0].reshape(GLA_KEY, GLA_DV)

    ri = lax.broadcasted_iota(I32, (CHUNK, CHUNK), 0)
    ci = lax.broadcasted_iota(I32, (CHUNK, CHUNK), 1)
    tri_low = jnp.where(ri >= ci, 1.0, 0.0).astype(BF16)
    tri_upp = jnp.where(ri <= ci, 1.0, 0.0).astype(BF16)
    rs = lax.broadcasted_iota(I32, (CHUNK, GLA_HEADS * CHUNK), 0)
    cs = lax.broadcasted_iota(I32, (CHUNK, GLA_HEADS * CHUNK), 1) % CHUNK
    mask_s[0] = jnp.where(rs >= cs, 1.0, 0.0)
    mask_s[1] = jnp.where(rs <= cs, 1.0, 0.0)
    lane_head = lax.broadcasted_iota(I32, (CHUNK, GLA_KEY), 1) // GLA_DK
    n = t // CHUNK

    group = min(GLA_GROUP, n // 2)

    def step(i, first):
        rows = []
        chains = []
        for j in range(group):
            c = i * group + j
            rf = pl.multiple_of(c * CHUNK, CHUNK)
            rb = pl.multiple_of((n - 1 - c) * CHUNK, CHUNK)
            rows += [rf, rb]
            chains.append((qk_ref[pl.ds(rf, CHUNK), :], v_ref[pl.ds(rf, CHUNK), :],
                           gf_s[pl.ds(rf, CHUNK), :], tri_low, mask_s[0], 0, CHUNK - 1))
            chains.append((qk_ref[pl.ds(rb, CHUNK), :], v_ref[pl.ds(rb, CHUNK), :],
                           gb_s[pl.ds(rb, CHUNK), :], tri_upp, mask_s[1], 1, 0))
        outs = _gla_group(chains, lane_head, s_ref, rhs_s)
        for r, o in zip(rows, outs):
            if first:
                o_ref[pl.ds(r, CHUNK), :] = o
            else:
                o_ref[pl.ds(r, CHUNK), :] += o

    def first_half(i, c):
        step(i, True)
        return c

    def second_half(i, c):
        step(i, False)
        return c

    half = n // (2 * group)
    lax.fori_loop(0, half, first_half, 0)
    lax.fori_loop(half, 2 * half, second_half, 0)
    if emit_state:
        sf_ref[0, 0] = s_ref[0].reshape(GLA_HEADS, GLA_DK, GLA_DV)
        sb_ref[0, 0] = s_ref[1].reshape(GLA_HEADS, GLA_DK, GLA_DV)


def _gla(proj, wdf, bdf, wdb, bdb, s0f, s0b, layer, bsz, t, emit_state):
    zero_init = s0f is None
    st_block = (1, 1, GLA_HEADS, GLA_DK, GLA_DV)
    in_specs = [
        pl.BlockSpec((t, COLBLK), lambda b: (b, 0)),
        pl.BlockSpec((t, COLBLK), lambda b: (b, 1)),
        pl.BlockSpec((t, LANES), lambda b: (b, Z_COLBLK)),
        pl.BlockSpec((1, LANES, GLA_KEY), lambda b: (layer, 0, 0)),
        pl.BlockSpec((1, 1, GLA_KEY), lambda b: (layer, 0, 0)),
        pl.BlockSpec((1, LANES, GLA_KEY), lambda b: (layer, 0, 0)),
        pl.BlockSpec((1, 1, GLA_KEY), lambda b: (layer, 0, 0)),
    ]
    args = [proj, proj, proj, wdf, bdf, wdb, bdb]
    if not zero_init:
        in_specs += [pl.BlockSpec(st_block, lambda b: (b, layer, 0, 0, 0))] * 2
        args += [s0f, s0b]
    out_specs = [pl.BlockSpec((t, MIX_GLA), lambda b: (b, 0))]
    out_shape = [jax.ShapeDtypeStruct((bsz * t, MIX_GLA), F32)]
    if emit_state:
        out_specs += [pl.BlockSpec((1, 1, GLA_HEADS, GLA_DK, GLA_DV), lambda b: (b, 0, 0, 0, 0))] * 2
        out_shape += [jax.ShapeDtypeStruct((bsz, 1, GLA_HEADS, GLA_DK, GLA_DV), F32)] * 2
    return pl.pallas_call(
        functools.partial(_gla_kernel, t=t, zero_init=zero_init, emit_state=emit_state),
        grid=(bsz,),
        in_specs=in_specs,
        out_specs=out_specs,
        out_shape=out_shape,
        scratch_shapes=[
            pltpu.VMEM((t, GLA_KEY), F32),
            pltpu.VMEM((t, GLA_KEY), F32),
            pltpu.VMEM((2, GLA_KEY, GLA_DV), F32),
            pltpu.VMEM((2, CHUNK, GLA_HEADS * CHUNK), F32),
            pltpu.VMEM((2 * min(GLA_GROUP, t // CHUNK // 2), 2 * GLA_KEY, MIX_GLA), BF16),
        ],
        compiler_params=_cparams(),
        name="gla",
    )(*args)


def _outproj_kernel(o_ref, r_ref, u_ref, x_ref, mod_ref, gn_ref, wo_ref, g2_ref, wr_ref,
                    xo_ref, h2_ref, aff_ref):
    gn = gn_ref[0]
    m = mod_ref[0]
    gt1 = m[:, 2 * D_MODEL:3 * D_MODEL]
    sh2 = m[:, 3 * D_MODEL:4 * D_MODEL]
    sc2 = m[:, 4 * D_MODEL:5 * D_MODEL]
    tm = o_ref.shape[0]
    subs = [slice(r0, r0 + OUTPROJ_SUB) for r0 in range(0, tm, OUTPROJ_SUB)]
    acts = []
    for rs in subs:
        o = o_ref[rs, :]
        r = r_ref[rs, :]
        parts = []
        for h in range(GLA_HEADS):
            hs = slice(h * GLA_DV, (h + 1) * GLA_DV)
            parts.append(((_rms(o[:, hs]) * gn) * _silu(r[:, hs])).astype(BF16))
        parts.append(u_ref[rs, :])
        acts.append(jnp.concatenate(parts, axis=1))
    mixed = [_dot(a, wo_ref[0]) for a in acts]
    h2s = []
    for rs, mix in zip(subs, mixed):
        x = x_ref[rs, :] + gt1 * mix
        xo_ref[rs, :] = x
        h2 = ((_rms(x) * g2_ref[0]) * (1.0 + sc2) + sh2).astype(BF16)
        h2_ref[rs, :] = h2
        h2s.append(h2)
    logit = [_dot(h2, wr_ref[0]) for h2 in h2s]
    for rs, logits in zip(subs, logit):
        lane = lax.broadcasted_iota(I32, logits.shape, 1)
        logits = jnp.where(lane < N_EXPERTS, logits, -jnp.inf)
        ex = jnp.exp(logits - jnp.max(logits, axis=-1, keepdims=True))
        aff_ref[rs, :] = ex / jnp.sum(ex, axis=-1, keepdims=True)


def _outproj(o, proj, u, x, modp, gla_norm_g, w_out_b, norm_ffn_g, w_router_b, layer, bsz, t):
    tm = min(t, 512)
    nt = t // tm
    row = lambda b, i: (b * nt + i, 0)
    return pl.pallas_call(
        _outproj_kernel,
        grid=(bsz, nt),
        in_specs=[
            pl.BlockSpec((tm, MIX_GLA), row),
            pl.BlockSpec((tm, COLBLK), lambda b, i: (b * nt + i, 2)),
            pl.BlockSpec((tm, MIX_CONV), row),
            pl.BlockSpec((tm, D_MODEL), row),
            pl.BlockSpec((1, 1, 6 * D_MODEL), lambda b, i: (b, 0, 0)),
            pl.BlockSpec((1, 1, GLA_DV), lambda b, i: (layer, 0, 0)),
            pl.BlockSpec((1, D_MODEL, D_MODEL), lambda b, i: (layer, 0, 0)),
            pl.BlockSpec((1, 1, D_MODEL), lambda b, i: (layer, 0, 0)),
            pl.BlockSpec((1, D_MODEL, LANES), lambda b, i: (layer, 0, 0)),
        ],
        out_specs=[
            pl.BlockSpec((tm, D_MODEL), row),
            pl.BlockSpec((tm, D_MODEL), row),
            pl.BlockSpec((tm, LANES), row),
        ],
        out_shape=[
            jax.ShapeDtypeStruct((bsz * t, D_MODEL), F32),
            jax.ShapeDtypeStruct((bsz * t, D_MODEL), BF16),
            jax.ShapeDtypeStruct((bsz * t, LANES), F32),
        ],
        compiler_params=_cparams(),
        name="outproj",
    )(o, proj, u, x, modp, gla_norm_g, w_out_b, norm_ffn_g, w_router_b)


PREFIX_BLK = 256


def _excl_prefix(x, t):
    blk = min(t, PREFIX_BLK)
    ri = lax.broadcasted_iota(I32, (blk, blk), 0)
    ci = lax.broadcasted_iota(I32, (blk, blk), 1)
    upper = jnp.where(ri < ci, 1.0, 0.0).astype(BF16)
    run = jnp.zeros((x.shape[0], 1), F32)
    outs = []
    for j in range(t // blk):
        xb = x[:, j * blk:(j + 1) * blk]
        outs.append(_dot(xb.astype(BF16), upper) + run)
        run = run + jnp.sum(xb, axis=1, keepdims=True)
    return jnp.concatenate(outs, axis=1) if len(outs) > 1 else outs[0]


F32_MAGNITUDE_BITS = 31
REFINE_STEPS = 16


def _route_kernel(aff_ref, sel_ref, selt_ref, afft_ref, at_s, *, bsz, t, cap):
    for b in range(bsz):
        at_s[b * N_EXPERTS:(b + 1) * N_EXPERTS, :] = aff_ref[b * t:(b + 1) * t, :].T[0:N_EXPERTS, :]
    a = at_s[...]
    afft_ref[...] = a
    rows = bsz * N_EXPERTS
    capf = float(cap)

    def enough(th):
        return jnp.sum(jnp.where(a >= th, 1.0, 0.0), axis=1, keepdims=True) >= capf

    def bit_step(i, thr):
        cand = thr | jnp.left_shift(jnp.int32(1), F32_MAGNITUDE_BITS - 1 - i)
        return jnp.where(enough(lax.bitcast_convert_type(cand, F32)), cand, thr)

    thr = lax.fori_loop(0, F32_MAGNITUDE_BITS, bit_step, jnp.zeros((rows, 1), I32))
    lo = lax.bitcast_convert_type(thr, F32)
    hi = lax.bitcast_convert_type(thr + 1, F32)

    def refine(i, c):
        lo, hi = c
        mid = lo + (hi - lo) * 0.5
        ok = enough(mid)
        return jnp.where(ok, mid, lo), jnp.where(ok, hi, mid)

    lo, hi = lax.fori_loop(0, REFINE_STEPS, refine, (lo, hi))
    gt = jnp.where(a >= hi, 1.0, 0.0) * jnp.where(a > lo, 1.0, 0.0)
    eq = jnp.where(a >= lo, 1.0, 0.0) - gt
    need = capf - jnp.sum(gt, axis=1, keepdims=True)
    tie_rank = _excl_prefix(eq, t)
    m = gt + eq * jnp.where(tie_rank < need, 1.0, 0.0)
    slot = _excl_prefix(m, t)
    sel = jnp.where(m > 0.5, slot.astype(I32), -1)
    sel_ref[...] = sel
    filler = jnp.full((LANES - N_EXPERTS, t), -1, I32)
    for b in range(bsz):
        selt_ref[b * t:(b + 1) * t, :] = jnp.concatenate(
            [sel[b * N_EXPERTS:(b + 1) * N_EXPERTS, :], filler], axis=0).T


def _route(aff, bsz, t, cap):
    rows = bsz * N_EXPERTS
    sel, selt, afft = pl.pallas_call(
        functools.partial(_route_kernel, bsz=bsz, t=t, cap=cap),
        grid=(1,),
        in_specs=[pl.BlockSpec((bsz * t, LANES), lambda i: (0, 0))],
        out_specs=[
            pl.BlockSpec((rows, t), lambda i: (0, 0)),
            pl.BlockSpec((bsz * t, LANES), lambda i: (0, 0)),
            pl.BlockSpec((rows, t), lambda i: (0, 0)),
        ],
        out_shape=[
            jax.ShapeDtypeStruct((rows, t), I32),
            jax.ShapeDtypeStruct((bsz * t, LANES), I32),
            jax.ShapeDtypeStruct((rows, t), F32),
        ],
        scratch_shapes=[pltpu.VMEM((rows, t), F32)],
        compiler_params=_cparams(),
        name="route",
    )(aff)
    return sel.reshape(bsz, N_EXPERTS, t), selt, afft.reshape(bsz, N_EXPERTS, t)


def _gather_kernel(h_ref, sel_ref, afft_ref, xs_ref, gate_ref, *, t, cap, eg):
    g = pl.program_id(1)
    hb = h_ref[...]
    slot = lax.broadcasted_iota(I32, (cap, t), 0)
    for el in range(eg):
        if eg == N_EXPERTS:
            srow = sel_ref[0, el:el + 1, :]
            arow = afft_ref[0, el:el + 1, :]
        else:
            e = g * eg + el
            srow = sel_ref[0, pl.ds(e, 1), :]
            arow = afft_ref[0, pl.ds(e, 1), :]
        hit = srow == slot
        p = jnp.where(hit, 1.0, 0.0).astype(BF16)
        xs_ref[el, 0] = _dot(p, hb).astype(BF16)
        gate_ref[el, 0] = jnp.sum(jnp.where(hit, arow, 0.0), axis=1, keepdims=True)


GATHER_SLOTS = 64


def _gather_win_kernel(h_ref, sel_ref, afft_ref, xs_ref, gate_ref, p_s, *, t, cap):
    nsub = cap // GATHER_SLOTS
    per = t // nsub
    win = 2 * per
    j = pl.program_id(1)
    s0 = j * GATHER_SLOTS
    lo = pl.multiple_of(jnp.clip(j * per - per // 2, 0, t - win), per // 2)
    sel = sel_ref[0]
    rel = sel - s0
    tok = lax.broadcasted_iota(I32, (1, t), 1)
    mine = jnp.where(rel < 0, 0.0, jnp.where(rel >= GATHER_SLOTS, 0.0, 1.0))
    outside = mine * jnp.where(tok < lo, 1.0, jnp.where(tok >= lo + win, 1.0, 0.0))
    windowed = jnp.max(outside) == 0.0
    slot = lax.broadcasted_iota(I32, (GATHER_SLOTS, 1), 0)

    def run(width, sel_w, aff_w, h_w):
        for e in range(N_EXPERTS):
            hit = (sel_w[e:e + 1, :] - s0) == slot
            p_s[e * GATHER_SLOTS:(e + 1) * GATHER_SLOTS, 0:width] = jnp.where(hit, 1.0, 0.0).astype(BF16)
            gate_ref[e, 0] = jnp.sum(jnp.where(hit, aff_w[e:e + 1, :], 0.0), axis=1, keepdims=True)
        xs = _dot(p_s[:, 0:width], h_w).astype(BF16)
        xs_ref[:, 0] = xs.reshape(N_EXPERTS, GATHER_SLOTS, D_MODEL)

    @pl.when(windowed)
    def _():
        run(win, sel_ref[0, :, pl.ds(lo, win)], afft_ref[0, :, pl.ds(lo, win)], h_ref[pl.ds(lo, win), :])

    @pl.when(jnp.logical_not(windowed))
    def _():
        run(t, sel, afft_ref[0], h_ref[...])


def _gather(h2, sel, afft, bsz, t, cap):
    if cap % GATHER_SLOTS == 0 and cap // GATHER_SLOTS > 1:
        return pl.pallas_call(
            functools.partial(_gather_win_kernel, t=t, cap=cap),
            grid=(bsz, cap // GATHER_SLOTS),
            in_specs=[
                pl.BlockSpec((t, D_MODEL), lambda b, j: (b, 0)),
                pl.BlockSpec((1, N_EXPERTS, t), lambda b, j: (b, 0, 0)),
                pl.BlockSpec((1, N_EXPERTS, t), lambda b, j: (b, 0, 0)),
            ],
            out_specs=[
                pl.BlockSpec((N_EXPERTS, 1, GATHER_SLOTS, D_MODEL), lambda b, j: (0, b, j, 0)),
                pl.BlockSpec((N_EXPERTS, 1, GATHER_SLOTS, 1), lambda b, j: (0, b, j, 0)),
            ],
            out_shape=[
                jax.ShapeDtypeStruct((N_EXPERTS, bsz, cap, D_MODEL), BF16),
                jax.ShapeDtypeStruct((N_EXPERTS, bsz, cap, 1), F32),
            ],
            scratch_shapes=[pltpu.VMEM((N_EXPERTS * GATHER_SLOTS, t), BF16)],
            compiler_params=_cparams(),
            name="gather",
        )(h2, sel, afft)
    eg = max(1, min(N_EXPERTS, 512 // cap))
    return pl.pallas_call(
        functools.partial(_gather_kernel, t=t, cap=cap, eg=eg),
        grid=(bsz, N_EXPERTS // eg),
        in_specs=[
            pl.BlockSpec((t, D_MODEL), lambda b, g: (b, 0)),
            pl.BlockSpec((1, N_EXPERTS, t), lambda b, g: (b, 0, 0)),
            pl.BlockSpec((1, N_EXPERTS, t), lambda b, g: (b, 0, 0)),
        ],
        out_specs=[
            pl.BlockSpec((eg, 1, cap, D_MODEL), lambda b, g: (g, b, 0, 0)),
            pl.BlockSpec((eg, 1, cap, 1), lambda b, g: (g, b, 0, 0)),
        ],
        out_shape=[
            jax.ShapeDtypeStruct((N_EXPERTS, bsz, cap, D_MODEL), BF16),
            jax.ShapeDtypeStruct((N_EXPERTS, bsz, cap, 1), F32),
        ],
        compiler_params=_cparams(),
        name="gather",
    )(h2, sel, afft)


FFN_TF = 256
FFN_TM = 512


def _ffn_kernel(xc_ref, xl_ref, gc_ref, gl_ref, wg_ref, wu_ref, wd_ref, yc_ref, yl_ref, acc_ref):
    def step(first):
        wg = wg_ref[0, 0].astype(BF16)
        wu = wu_ref[0, 0].astype(BF16)
        wd = wd_ref[0, 0].astype(BF16)
        base = 0
        for x_ref, g_ref, y_ref in ((xc_ref, gc_ref, yc_ref), (xl_ref, gl_ref, yl_ref)):
            rows = x_ref.shape[1]
            for r0 in range(0, rows, FFN_TM):
                x = x_ref[0, r0:r0 + FFN_TM, :]
                hm = (_silu(_dot(x, wg)) * _dot(x, wu)).astype(BF16)
                part = _dot(hm, wd)
                if first:
                    acc_ref[base + r0:base + r0 + FFN_TM, :] = part
                else:
                    acc = acc_ref[base + r0:base + r0 + FFN_TM, :] + part
                    acc_ref[base + r0:base + r0 + FFN_TM, :] = acc
                    y_ref[0, r0:r0 + FFN_TM, :] = (acc * g_ref[0, r0:r0 + FFN_TM, :]).astype(BF16)
            base += rows

    j = pl.program_id(1)
    pl.when(j == 0)(functools.partial(step, True))
    pl.when(j > 0)(functools.partial(step, False))


def _ffn(xs_c, gates_c, xs_l, gates_l, w_e_gate, w_e_up, w_e_down, layer):
    rc = xs_c.shape[1]
    rl = xs_l.shape[1]
    assert rc % FFN_TM == 0 and rl % FFN_TM == 0
    rowblk = lambda r, w: pl.BlockSpec((1, r, w), lambda e, j: (e, 0, 0))
    return pl.pallas_call(
        _ffn_kernel,
        grid=(N_EXPERTS, D_EXPERT // FFN_TF),
        in_specs=[
            rowblk(rc, D_MODEL), rowblk(rl, D_MODEL), rowblk(rc, 1), rowblk(rl, 1),
            pl.BlockSpec((1, 1, D_MODEL, FFN_TF), lambda e, j: (layer, e, 0, j)),
            pl.BlockSpec((1, 1, D_MODEL, FFN_TF), lambda e, j: (layer, e, 0, j)),
            pl.BlockSpec((1, 1, FFN_TF, D_MODEL), lambda e, j: (layer, e, j, 0)),
        ],
        out_specs=[rowblk(rc, D_MODEL), rowblk(rl, D_MODEL)],
        out_shape=[
            jax.ShapeDtypeStruct((N_EXPERTS, rc, D_MODEL), BF16),
            jax.ShapeDtypeStruct((N_EXPERTS, rl, D_MODEL), BF16),
        ],
        scratch_shapes=[pltpu.VMEM((rc + rl, D_MODEL), F32)],
        compiler_params=_cparams(),
        name="ffn",
    )(xs_c, xs_l, gates_c, gates_l, w_e_gate, w_e_up, w_e_down)


def _scatter_kernel(*refs, cap, nt, final):
    if final:
        selt_ref, y_ref, x_ref, mod_ref, gf_ref, xo_ref, pt_s, yw_s = refs
    else:
        selt_ref, y_ref, x_ref, mod_ref, xo_ref, pt_s, yw_s = refs
    tt = selt_ref.shape[0]
    ncol = N_EXPERTS * cap
    gw = min(ncol, 1024)
    gt2 = mod_ref[0][:, 5 * D_MODEL:6 * D_MODEL]

    def finish(moe):
        x = x_ref[...] + gt2 * moe
        if final:
            x = _rms(x) * gf_ref[...]
        xo_ref[...] = x

    if nt > 1:
        per = cap // nt
        win = 2 * per
        lo = pl.multiple_of(jnp.clip(pl.program_id(1) * per - per // 2, 0, cap - win), per // 2)
        st = selt_ref[...]
        rel = st - lo
        outside = jnp.where(st < 0, 0.0, jnp.where(rel < 0, 1.0, jnp.where(rel >= win, 1.0, 0.0)))
        windowed = jnp.max(outside) == 0.0

        @pl.when(windowed)
        def _():
            slot = lax.broadcasted_iota(I32, (1, win), 1)
            for e in range(N_EXPERTS):
                pt_s[:, e * win:(e + 1) * win] = jnp.where(rel[:, e:e + 1] == slot, 1.0, 0.0).astype(BF16)
                yw_s[e * win:(e + 1) * win, :] = y_ref[e, 0, pl.ds(lo, win), :]
            finish(_dot(pt_s[:, 0:N_EXPERTS * win], yw_s[...]))

        @pl.when(jnp.logical_not(windowed))
        def _():
            slot = lax.broadcasted_iota(I32, (1, cap), 1)
            for e in range(N_EXPERTS):
                pt_s[:, e * cap:(e + 1) * cap] = jnp.where(st[:, e:e + 1] == slot, 1.0, 0.0).astype(BF16)
            finish(_dot(pt_s[...], y_ref[:, 0].reshape(ncol, D_MODEL)))

        return
    if cap % LANES == 0:
        st = selt_ref[...]
        slot = lax.broadcasted_iota(I32, (1, cap), 1)
        for e in range(N_EXPERTS):
            pt_s[:, e * cap:(e + 1) * cap] = jnp.where(st[:, e:e + 1] == slot, 1.0, 0.0).astype(BF16)
    else:
        stf = selt_ref[...].astype(F32).astype(BF16)
        col = lax.broadcasted_iota(I32, (LANES, gw), 1)
        row = lax.broadcasted_iota(I32, (LANES, gw), 0)
        cmod = (lax.broadcasted_iota(I32, (1, gw), 1) % cap).astype(F32)
        for g in range(ncol // gw):
            expand = jnp.where((col + g * gw) // cap == row, 1.0, 0.0).astype(BF16)
            selx = _dot(stf, expand)
            pt_s[:, g * gw:(g + 1) * gw] = jnp.where(selx == cmod, 1.0, 0.0).astype(BF16)
    finish(_dot(pt_s[...], y_ref[:, 0].reshape(ncol, D_MODEL)))


def _scatter(selt, y, x, modp, norm_final_g, bsz, t, cap):
    final = norm_final_g is not None
    tt = min(t, 512)
    nt = t // tt
    row = lambda b, i: (b * nt + i, 0)
    in_specs = [
        pl.BlockSpec((tt, LANES), row),
        pl.BlockSpec((N_EXPERTS, 1, cap, D_MODEL), lambda b, i: (0, b, 0, 0)),
        pl.BlockSpec((tt, D_MODEL), row),
        pl.BlockSpec((1, 1, 6 * D_MODEL), lambda b, i: (b, 0, 0)),
    ]
    args = [selt, y, x, modp]
    if final:
        in_specs.append(pl.BlockSpec((1, D_MODEL), lambda b, i: (0, 0)))
        args.append(norm_final_g)
    return pl.pallas_call(
        functools.partial(_scatter_kernel, cap=cap, nt=nt, final=final),
        grid=(bsz, nt),
        in_specs=in_specs,
        out_specs=pl.BlockSpec((tt, D_MODEL), row),
        out_shape=jax.ShapeDtypeStruct((bsz * t, D_MODEL), F32),
        scratch_shapes=[
            pltpu.VMEM((tt, N_EXPERTS * cap), BF16),
            pltpu.VMEM((N_EXPERTS * 2 * cap // nt, D_MODEL) if nt > 1 else (2 * SUBLANES, LANES), BF16),
        ],
        compiler_params=_cparams(),
        name="scatter",
    )(*args)


def _mixer_and_route(x, modp, s0f, s0b, wts, layer, bsz, t, rlen, emit_state):
    cap = EC_FACTOR * t // N_EXPERTS
    proj, u = _inproj_conv(x, modp, wts["norm_mix_g"], wts["w_in"], wts["conv_w"], wts["conv_b"],
                           wts["conv_ln_g"], wts["conv_ln_b"], layer, bsz, t, rlen)
    gla = _gla(proj, wts["wdf"], wts["bdf"], wts["wdb"], wts["bdb"], s0f, s0b, layer, bsz, t, emit_state)
    if emit_state:
        o, sf, sb = gla
    else:
        (o,) = gla
        sf = sb = None
    x, h2, aff = _outproj(o, proj, u, x, modp, wts["gla_norm_g"], wts["w_out"], wts["norm_ffn_g"],
                          wts["w_router"], layer, bsz, t)
    sel, selt, afft = _route(aff, bsz, t, cap)
    xs, gates = _gather(h2, sel, afft, bsz, t, cap)
    return (x, selt, xs.reshape(N_EXPERTS, bsz * cap, D_MODEL), gates.reshape(N_EXPERTS, bsz * cap, 1)), sf, sb


def kernel(x_prompt, x_sample, state_gla_fwd, state_gla_bwd, c, c_ctx, norm_mix_g, norm_ffn_g, norm_final_g, w_mod, b_mod, w_in, w_decay_f, b_decay_f, w_decay_b, b_decay_b, gla_norm_g, conv_w, conv_b, conv_ln_g, conv_ln_b, w_out, w_router, w_e_gate, w_e_up, w_e_down):
    depth = w_in.shape[0]
    bp, tp, _ = x_prompt.shape
    bs, ts, _ = x_sample.shape

    zcols = 2 * GLA_KEY + 2 * MIX_GLA
    w_in_r = jnp.concatenate(
        [w_in[:, :, :zcols + 2 * GATE_RANK], jnp.zeros((depth, D_MODEL, LANES - 2 * GATE_RANK), F32),
         w_in[:, :, zcols + 2 * GATE_RANK:]], axis=2).astype(BF16)
    zpad_f = jnp.zeros((depth, LANES - GATE_RANK, GLA_KEY), F32)
    zpad_b0 = jnp.zeros((depth, GATE_RANK, GLA_KEY), F32)
    zpad_b1 = jnp.zeros((depth, LANES - 2 * GATE_RANK, GLA_KEY), F32)
    vec = lambda a: a.reshape(depth, 1, a.shape[-1])
    wts = dict(
        norm_mix_g=vec(norm_mix_g), norm_ffn_g=vec(norm_ffn_g), w_in=w_in_r,
        wdf=jnp.concatenate([w_decay_f, zpad_f], axis=1).astype(BF16), bdf=vec(b_decay_f),
        wdb=jnp.concatenate([zpad_b0, w_decay_b, zpad_b1], axis=1).astype(BF16), bdb=vec(b_decay_b),
        gla_norm_g=vec(gla_norm_g),
        conv_w=jnp.broadcast_to(
            conv_w.reshape(depth, CONV_W, 1, MIX_CONV // LANES, LANES).transpose(0, 3, 1, 2, 4),
            (depth, MIX_CONV // LANES, CONV_W, SUBLANES, LANES)),
        conv_b=jnp.broadcast_to(conv_b.reshape(depth, MIX_CONV // LANES, 1, LANES),
                                (depth, MIX_CONV // LANES, SUBLANES, LANES)),
        conv_ln_g=vec(conv_ln_g), conv_ln_b=vec(conv_ln_b), w_out=w_out.astype(BF16),
        w_router=jnp.concatenate(
            [w_router, jnp.zeros((depth, D_MODEL, LANES - N_EXPERTS), F32)], axis=2).astype(BF16),
        w_e_gate=w_e_gate, w_e_up=w_e_up, w_e_down=w_e_down,
    )

    nrow = 2 * SUBLANES
    cond = jnp.concatenate([c, c_ctx[None, :], jnp.zeros((nrow - bs - 1, D_MODEL), F32)], axis=0)
    mod = _modulation(cond, w_mod, b_mod)

    xp = x_prompt.reshape(bp * tp, D_MODEL)
    xs = x_sample.reshape(bs * ts, D_MODEL)
    final_g = norm_final_g.reshape(1, D_MODEL)
    fwd_states = []
    bwd_states = []
    for l in range(depth):
        last = l == depth - 1
        mod_ctx = jnp.broadcast_to(mod[l, bs][None, None, :], (bp, 1, 6 * D_MODEL))
        mod_lat = mod[l, :bs][:, None, :]
        (xp, selt_c, xs_c, gates_c), sf, sb = _mixer_and_route(
            xp, mod_ctx, None, None, wts, l, bp, tp, tp, True)
        fwd_states.append(sf)
        bwd_states.append(sb)
        (xs, selt_l, xs_l, gates_l), _, _ = _mixer_and_route(
            xs, mod_lat, state_gla_fwd, state_gla_bwd, wts, l, bs, ts, GRID_W, False)
        y_c, y_l = _ffn(xs_c, gates_c, xs_l, gates_l, wts["w_e_gate"], wts["w_e_up"], wts["w_e_down"], l)
        cap_c = EC_FACTOR * tp // N_EXPERTS
        cap_l = EC_FACTOR * ts // N_EXPERTS
        fin = final_g if last else None
        xp = _scatter(selt_c, y_c.reshape(N_EXPERTS, bp, cap_c, D_MODEL), xp, mod_ctx, fin, bp, tp, cap_c)
        xs = _scatter(selt_l, y_l.reshape(N_EXPERTS, bs, cap_l, D_MODEL), xs, mod_lat, fin, bs, ts, cap_l)
    y_prompt = xp.reshape(bp, tp, D_MODEL)
    y_sample = xs.reshape(bs, ts, D_MODEL)
    return (y_prompt, y_sample, jnp.concatenate(fwd_states, axis=1), jnp.concatenate(bwd_states, axis=1))
```

```python
import functools

import jax
import jax.numpy as jnp
from jax import lax
from jax.experimental import pallas as pl
from jax.experimental.pallas import tpu as pltpu

F32 = jnp.float32
BF16 = jnp.bfloat16
I32 = jnp.int32

D_MODEL = 1024
GRID_W = 64
MIX_GLA = 512
MIX_CONV = 512
GLA_HEADS = 4
GLA_DK = 64
GLA_DV = 128
GLA_KEY = GLA_HEADS * GLA_DK
GATE_RANK = 16
GATE_TAU = 16.0
CHUNK = 64
CONV_W = 31
CONV_HALF = CONV_W // 2
N_EXPERTS = 16
EC_FACTOR = 2
D_EXPERT = 1024
EPS = 1e-6

LANES = 128
SUBLANES = 8
BF16_ROWS = 16

PROJ_KEEP = 2 * GLA_KEY + 2 * MIX_GLA + LANES
PROJ_COLS = PROJ_KEEP + 2 * MIX_CONV
COLBLK = 512
Z_COLBLK = (2 * GLA_KEY + 2 * MIX_GLA) // LANES

VMEM_LIMIT = 56 * 1024 * 1024
CONV_PAD = 16
CONV_CH = 64
OUTPROJ_SUB = 128
GLA_GROUP = 4


def _cparams():
    return pltpu.CompilerParams(vmem_limit_bytes=VMEM_LIMIT)


def _silu(x):
    return x * jax.nn.sigmoid(x)


def _rms(x):
    return x * lax.rsqrt(jnp.mean(x * x, axis=-1, keepdims=True) + EPS)


def _dot(a, b):
    return jnp.dot(a, b, preferred_element_type=F32)


def _mod_kernel(c_ref, w_ref, b_ref, o_ref):
    s = _silu(c_ref[...]).astype(BF16)
    o_ref[0] = _dot(s, w_ref[0].astype(BF16)) + b_ref[0]


def _modulation(cond, w_mod, b_mod):
    depth = w_mod.shape[0]
    n_out = w_mod.shape[2]
    tn = 1536
    rows = cond.shape[0]
    return pl.pallas_call(
        _mod_kernel,
        grid=(depth, n_out // tn),
        in_specs=[
            pl.BlockSpec((rows, D_MODEL), lambda l, j: (0, 0)),
            pl.BlockSpec((1, D_MODEL, tn), lambda l, j: (l, 0, j)),
            pl.BlockSpec((1, 1, tn), lambda l, j: (l, 0, j)),
        ],
        out_specs=pl.BlockSpec((1, rows, tn), lambda l, j: (l, 0, j)),
        out_shape=jax.ShapeDtypeStruct((depth, rows, n_out), F32),
        compiler_params=_cparams(),
        name="modulation",
    )(cond, w_mod, b_mod.reshape(depth, 1, n_out))


def _inproj_conv_kernel(x_ref, mod_ref, g_ref, w_ref, cw_ref, cb_ref, lg_ref, lb_ref,
                        proj_ref, u_ref, pad_ref, y_ref, *, nrows, rlen):
    stride = rlen + 2 * CONV_PAD
    ntile = MIX_CONV // LANES
    tb = nrows * rlen

    @pl.when(pl.program_id(0) == 0)
    def _():
        pad_ref[...] = jnp.zeros(pad_ref.shape, F32)

    assert rlen == CONV_CH or nrows == 1
    pstep = stride if rlen == CONV_CH else CONV_CH
    off = CONV_PAD - CONV_HALF
    ngrp = CONV_CH // SUBLANES
    nshift = (CONV_CH + 2 * CONV_PAD) // SUBLANES - 1
    for c in range(tb // CONV_CH):
        for lt in range(ntile):
            win = pad_ref[lt, c * pstep:c * pstep + CONV_CH + 2 * CONV_PAD, :]
            acc = None
            for s in range(SUBLANES):
                xs = win if s == 0 else win[s:s + nshift * SUBLANES, :]
                xs = xs.reshape(-1, SUBLANES, LANES)
                for a in range(2 * CONV_PAD // SUBLANES):
                    j = SUBLANES * a + s - off
                    if 0 <= j < CONV_W:
                        term = xs[a:a + ngrp] * cw_ref[0, lt, j][None]
                        acc = term if acc is None else acc + term
            y_ref[lt, c * ngrp:(c + 1) * ngrp] = acc + cb_ref[0, lt][None]
    y = jnp.concatenate([y_ref[lt].reshape(tb, LANES) for lt in range(ntile)], axis=1)
    yc = y - jnp.mean(y, axis=-1, keepdims=True)
    yn = yc * lax.rsqrt(jnp.mean(yc * yc, axis=-1, keepdims=True) + EPS)
    u_ref[...] = _silu(yn * lg_ref[0] + lb_ref[0]).astype(BF16)

    m = mod_ref[0]
    sh = m[:, 0:D_MODEL]
    sc = m[:, D_MODEL:2 * D_MODEL]
    h = (_rms(x_ref[...]) * g_ref[0]) * (1.0 + sc) + sh
    p = _dot(h.astype(BF16), w_ref[0])
    proj_ref[...] = p[:, 0:PROJ_KEEP]
    glu = p[:, PROJ_KEEP:PROJ_KEEP + MIX_CONV] * jax.nn.sigmoid(p[:, PROJ_KEEP + MIX_CONV:PROJ_COLS])
    for lt in range(ntile):
        for r in range(nrows):
            base = r * stride + CONV_PAD
            pad_ref[lt, base:base + rlen, :] = glu[r * rlen:(r + 1) * rlen, lt * LANES:(lt + 1) * LANES]


def _inproj_conv(x, modp, norm_g, w_in_b, conv_w, conv_b, ln_g, ln_b, layer, bsz, t, rlen):
    tb = max(rlen, min(t, 512))
    nrows = tb // rlen
    nt = t // tb
    last = bsz * nt - 1
    ntile = MIX_CONV // LANES
    cur = lambda s: jnp.minimum(s, last)
    prev = lambda s: jnp.maximum(s - 1, 0)
    vec = pl.BlockSpec((1, 1, MIX_CONV), lambda s: (layer, 0, 0))
    return pl.pallas_call(
        functools.partial(_inproj_conv_kernel, nrows=nrows, rlen=rlen),
        grid=(bsz * nt + 1,),
        in_specs=[
            pl.BlockSpec((tb, D_MODEL), lambda s: (cur(s), 0)),
            pl.BlockSpec((1, 1, 6 * D_MODEL), lambda s: (cur(s) // nt, 0, 0)),
            pl.BlockSpec((1, 1, D_MODEL), lambda s: (layer, 0, 0)),
            pl.BlockSpec((1, D_MODEL, PROJ_COLS), lambda s: (layer, 0, 0)),
            pl.BlockSpec((1, ntile, CONV_W, SUBLANES, LANES), lambda s: (layer, 0, 0, 0, 0)),
            pl.BlockSpec((1, ntile, SUBLANES, LANES), lambda s: (layer, 0, 0, 0)),
            vec, vec,
        ],
        out_specs=[
            pl.BlockSpec((tb, PROJ_KEEP), lambda s: (cur(s), 0)),
            pl.BlockSpec((tb, MIX_CONV), lambda s: (prev(s), 0)),
        ],
        out_shape=[
            jax.ShapeDtypeStruct((bsz * t, PROJ_KEEP), F32),
            jax.ShapeDtypeStruct((bsz * t, MIX_CONV), BF16),
        ],
        scratch_shapes=[
            pltpu.VMEM((ntile, nrows * (rlen + 2 * CONV_PAD), LANES), F32),
            pltpu.VMEM((ntile, tb // SUBLANES, SUBLANES, LANES), F32),
        ],
        compiler_params=_cparams(),
        name="inproj_conv",
    )(x, modp, norm_g, w_in_b, conv_w, conv_b, ln_g, ln_b)


def _log_sigmoid(x):
    return jnp.minimum(x, 0.0) - jnp.log(1.0 + jnp.exp(-jnp.abs(x)))


def _head_stack(x, lane_head):
    return jnp.concatenate([jnp.where(lane_head == h, x, 0.0) for h in range(GLA_HEADS)], axis=0).astype(BF16)


def _gla_group(chains, lane_head, s_ref, rhs_ref):
    cums = []
    for qk, v, g, tri, causal, d, last_row in chains:
        g_hi = g.astype(BF16)
        g_lo = (g - g_hi.astype(F32)).astype(BF16)
        cums.append(_dot(tri, g_hi) + _dot(tri, g_lo))
    pre = []
    for (qk, v, g, tri, causal, d, last_row), cum in zip(chains, cums):
        cl = cum[last_row:last_row + 1, :]
        q = qk[:, 0:GLA_KEY]
        k = qk[:, GLA_KEY:2 * GLA_KEY]
        qd = (q * (jnp.exp(cum) * (GLA_DK ** -0.5))).astype(BF16)
        k_inv = k * jnp.exp(-cum)
        k_end = k_inv * jnp.exp(cl)
        tr = jnp.concatenate([k_end, cum], axis=0).T
        pre.append((qd, _head_stack(k_inv, lane_head), tr, v.astype(BF16)))
    mm = []
    for (qk, v, g, tri, causal, d, last_row), (qd, ks, tr, vb) in zip(chains, pre):
        sc = lax.dot_general(qd, ks, (((1,), (1,)), ((), ())), preferred_element_type=F32)
        upd = [_dot(tr[h * GLA_DK:(h + 1) * GLA_DK, 0:CHUNK].astype(BF16),
                    vb[:, h * GLA_DV:(h + 1) * GLA_DV]) for h in range(GLA_HEADS)]
        mm.append((sc, jnp.concatenate(upd, axis=0)))
    state = {}
    lhs = []
    for i, ((qk, v, g, tri, causal, d, last_row), (qd, ks, tr, vb), (sc, upd)) in enumerate(zip(chains, pre, mm)):
        s_prev = state[d] if d in state else s_ref[d]
        s_prev_b = s_prev.astype(BF16)
        for h in range(GLA_HEADS):
            rows = slice(h * GLA_DK, (h + 1) * GLA_DK)
            cols = slice(h * GLA_DV, (h + 1) * GLA_DV)
            rhs_ref[i, rows, cols] = vb[:, cols]
            rhs_ref[i, GLA_KEY + h * GLA_DK:GLA_KEY + (h + 1) * GLA_DK, cols] = s_prev_b[rows, :]
        decay = jnp.exp(tr[:, CHUNK + last_row:CHUNK + last_row + 1])
        state[d] = decay * s_prev + upd
        lhs.append(jnp.concatenate([(sc * causal).astype(BF16), qd], axis=1))
    outs = [_dot(l, rhs_ref[i]) for i, l in enumerate(lhs)]
    for d, s_new in state.items():
        s_ref[d] = s_new
    return outs


def _gla_kernel(*refs, t, zero_init, emit_state):
    qk_ref, v_ref, z_ref, wf_ref, bf_ref, wb_ref, bb_ref = refs[:7]
    pos = 7
    if not zero_init:
        s0f_ref, s0b_ref = refs[pos:pos + 2]
        pos += 2
    o_ref = refs[pos]
    pos += 1
    if emit_state:
        sf_ref, sb_ref = refs[pos:pos + 2]
        pos += 2
    gf_s, gb_s, s_ref, mask_s, rhs_s = refs[pos:pos + 5]
    rhs_s[...] = jnp.zeros(rhs_s.shape, BF16)

    z = z_ref[...].astype(BF16)
    gf_s[...] = _log_sigmoid(_dot(z, wf_ref[0]) + bf_ref[0]) * (1.0 / GATE_TAU)
    gb_s[...] = _log_sigmoid(_dot(z, wb_ref[0]) + bb_ref[0]) * (1.0 / GATE_TAU)
    if zero_init:
        s_ref[...] = jnp.zeros(s_ref.shape, F32)
    else:
        s_ref[0] = s0f_ref[0, 0].reshape(GLA_KEY, GLA_DV)
        s_ref[1] = s0b_ref[0, 0].reshape(GLA_KEY, GLA_DV)

    ri = lax.broadcasted_iota(I32, (CHUNK, CHUNK), 0)
    ci = lax.broadcasted_iota(I32, (CHUNK, CHUNK), 1)
    tri_low = jnp.where(ri >= ci, 1.0, 0.0).astype(BF16)
    tri_upp = jnp.where(ri <= ci, 1.0, 0.0).astype(BF16)
    rs = lax.broadcasted_iota(I32, (CHUNK, GLA_HEADS * CHUNK), 0)
    cs = lax.broadcasted_iota(I32, (CHUNK, GLA_HEADS * CHUNK), 1) % CHUNK
    mask_s[0] = jnp.where(rs >= cs, 1.0, 0.0)
    mask_s[1] = jnp.where(rs <= cs, 1.0, 0.0)
    lane_head = lax.broadcasted_iota(I32, (CHUNK, GLA_KEY), 1) // GLA_DK
    n = t // CHUNK

    group = min(GLA_GROUP, n // 2)

    def step(i, first):
        rows = []
        chains = []
        for j in range(group):
            c = i * group + j
            rf = pl.multiple_of(c * CHUNK, CHUNK)
            rb = pl.multiple_of((n - 1 - c) * CHUNK, CHUNK)
            rows += [rf, rb]
            chains.append((qk_ref[pl.ds(rf, CHUNK), :], v_ref[pl.ds(rf, CHUNK), :],
                           gf_s[pl.ds(rf, CHUNK), :], tri_low, mask_s[0], 0, CHUNK - 1))
            chains.append((qk_ref[pl.ds(rb, CHUNK), :], v_ref[pl.ds(rb, CHUNK), :],
                           gb_s[pl.ds(rb, CHUNK), :], tri_upp, mask_s[1], 1, 0))
        outs = _gla_group(chains, lane_head, s_ref, rhs_s)
        for r, o in zip(rows, outs):
            if first:
                o_ref[pl.ds(r, CHUNK), :] = o
            else:
                o_ref[pl.ds(r, CHUNK), :] += o

    def first_half(i, c):
        step(i, True)
        return c

    def second_half(i, c):
        step(i, False)
        return c

    half = n // (2 * group)
    lax.fori_loop(0, half, first_half, 0)
    lax.fori_loop(half, 2 * half, second_half, 0)
    if emit_state:
        sf_ref[0, 0] = s_ref[0].reshape(GLA_HEADS, GLA_DK, GLA_DV)
        sb_ref[0, 0] = s_ref[1].reshape(GLA_HEADS, GLA_DK, GLA_DV)


def _gla(proj, wdf, bdf, wdb, bdb, s0f, s0b, layer, bsz, t, emit_state):
    zero_init = s0f is None
    st_block = (1, 1, GLA_HEADS, GLA_DK, GLA_DV)
    in_specs = [
        pl.BlockSpec((t, COLBLK), lambda b: (b, 0)),
        pl.BlockSpec((t, COLBLK), lambda b: (b, 1)),
        pl.BlockSpec((t, LANES), lambda b: (b, Z_COLBLK)),
        pl.BlockSpec((1, LANES, GLA_KEY), lambda b: (layer, 0, 0)),
        pl.BlockSpec((1, 1, GLA_KEY), lambda b: (layer, 0, 0)),
        pl.BlockSpec((1, LANES, GLA_KEY), lambda b: (layer, 0, 0)),
        pl.BlockSpec((1, 1, GLA_KEY), lambda b: (layer, 0, 0)),
    ]
    args = [proj, proj, proj, wdf, bdf, wdb, bdb]
    if not zero_init:
        in_specs += [pl.BlockSpec(st_block, lambda b: (b, layer, 0, 0, 0))] * 2
        args += [s0f, s0b]
    out_specs = [pl.BlockSpec((t, MIX_GLA), lambda b: (b, 0))]
    out_shape = [jax.ShapeDtypeStruct((bsz * t, MIX_GLA), F32)]
    if emit_state:
        out_specs += [pl.BlockSpec((1, 1, GLA_HEADS, GLA_DK, GLA_DV), lambda b: (b, 0, 0, 0, 0))] * 2
        out_shape += [jax.ShapeDtypeStruct((bsz, 1, GLA_HEADS, GLA_DK, GLA_DV), F32)] * 2
    return pl.pallas_call(
        functools.partial(_gla_kernel, t=t, zero_init=zero_init, emit_state=emit_state),
        grid=(bsz,),
        in_specs=in_specs,
        out_specs=out_specs,
        out_shape=out_shape,
        scratch_shapes=[
            pltpu.VMEM((t, GLA_KEY), F32),
            pltpu.VMEM((t, GLA_KEY), F32),
            pltpu.VMEM((2, GLA_KEY, GLA_DV), F32),
            pltpu.VMEM((2, CHUNK, GLA_HEADS * CHUNK), F32),
            pltpu.VMEM((2 * min(GLA_GROUP, t // CHUNK // 2), 2 * GLA_KEY, MIX_GLA), BF16),
        ],
        compiler_params=_cparams(),
        name="gla",
    )(*args)


def _outproj_kernel(o_ref, r_ref, u_ref, x_ref, mod_ref, gn_ref, wo_ref, g2_ref, wr_ref,
                    xo_ref, h2_ref, aff_ref):
    gn = gn_ref[0]
    m = mod_ref[0]
    gt1 = m[:, 2 * D_MODEL:3 * D_MODEL]
    sh2 = m[:, 3 * D_MODEL:4 * D_MODEL]
    sc2 = m[:, 4 * D_MODEL:5 * D_MODEL]
    tm = o_ref.shape[0]
    subs = [slice(r0, r0 + OUTPROJ_SUB) for r0 in range(0, tm, OUTPROJ_SUB)]
    acts = []
    for rs in subs:
        o = o_ref[rs, :]
        r = r_ref[rs, :]
        parts = []
        for h in range(GLA_HEADS):
            hs = slice(h * GLA_DV, (h + 1) * GLA_DV)
            parts.append(((_rms(o[:, hs]) * gn) * _silu(r[:, hs])).astype(BF16))
        parts.append(u_ref[rs, :])
        acts.append(jnp.concatenate(parts, axis=1))
    mixed = [_dot(a, wo_ref[0]) for a in acts]
    h2s = []
    for rs, mix in zip(subs, mixed):
        x = x_ref[rs, :] + gt1 * mix
        xo_ref[rs, :] = x
        h2 = ((_rms(x) * g2_ref[0]) * (1.0 + sc2) + sh2).astype(BF16)
        h2_ref[rs, :] = h2
        h2s.append(h2)
    logit = [_dot(h2, wr_ref[0]) for h2 in h2s]
    for rs, logits in zip(subs, logit):
        lane = lax.broadcasted_iota(I32, logits.shape, 1)
        logits = jnp.where(lane < N_EXPERTS, logits, -jnp.inf)
        ex = jnp.exp(logits - jnp.max(logits, axis=-1, keepdims=True))
        aff_ref[rs, :] = ex / jnp.sum(ex, axis=-1, keepdims=True)


def _outproj(o, proj, u, x, modp, gla_norm_g, w_out_b, norm_ffn_g, w_router_b, layer, bsz, t):
    tm = min(t, 512)
    nt = t // tm
    row = lambda b, i: (b * nt + i, 0)
    return pl.pallas_call(
        _outproj_kernel,
        grid=(bsz, nt),
        in_specs=[
            pl.BlockSpec((tm, MIX_GLA), row),
            pl.BlockSpec((tm, COLBLK), lambda b, i: (b * nt + i, 2)),
            pl.BlockSpec((tm, MIX_CONV), row),
            pl.BlockSpec((tm, D_MODEL), row),
            pl.BlockSpec((1, 1, 6 * D_MODEL), lambda b, i: (b, 0, 0)),
            pl.BlockSpec((1, 1, GLA_DV), lambda b, i: (layer, 0, 0)),
            pl.BlockSpec((1, D_MODEL, D_MODEL), lambda b, i: (layer, 0, 0)),
            pl.BlockSpec((1, 1, D_MODEL), lambda b, i: (layer, 0, 0)),
            pl.BlockSpec((1, D_MODEL, LANES), lambda b, i: (layer, 0, 0)),
        ],
        out_specs=[
            pl.BlockSpec((tm, D_MODEL), row),
            pl.BlockSpec((tm, D_MODEL), row),
            pl.BlockSpec((tm, LANES), row),
        ],
        out_shape=[
            jax.ShapeDtypeStruct((bsz * t, D_MODEL), F32),
            jax.ShapeDtypeStruct((bsz * t, D_MODEL), BF16),
            jax.ShapeDtypeStruct((bsz * t, LANES), F32),
        ],
        compiler_params=_cparams(),
        name="outproj",
    )(o, proj, u, x, modp, gla_norm_g, w_out_b, norm_ffn_g, w_router_b)


PREFIX_BLK = 256


def _excl_prefix(x, t):
    blk = min(t, PREFIX_BLK)
    ri = lax.broadcasted_iota(I32, (blk, blk), 0)
    ci = lax.broadcasted_iota(I32, (blk, blk), 1)
    upper = jnp.where(ri < ci, 1.0, 0.0).astype(BF16)
    run = jnp.zeros((x.shape[0], 1), F32)
    outs = []
    for j in range(t // blk):
        xb = x[:, j * blk:(j + 1) * blk]
        outs.append(_dot(xb.astype(BF16), upper) + run)
        run = run + jnp.sum(xb, axis=1, keepdims=True)
    return jnp.concatenate(outs, axis=1) if len(outs) > 1 else outs[0]


F32_MAGNITUDE_BITS = 31
REFINE_STEPS = 16


def _route_kernel(aff_ref, sel_ref, selt_ref, afft_ref, at_s, *, bsz, t, cap):
    for b in range(bsz):
        at_s[b * N_EXPERTS:(b + 1) * N_EXPERTS, :] = aff_ref[b * t:(b + 1) * t, :].T[0:N_EXPERTS, :]
    a = at_s[...]
    afft_ref[...] = a
    rows = bsz * N_EXPERTS
    capf = float(cap)

    def enough(th):
        return jnp.sum(jnp.where(a >= th, 1.0, 0.0), axis=1, keepdims=True) >= capf

    def bit_step(i, thr):
        cand = thr | jnp.left_shift(jnp.int32(1), F32_MAGNITUDE_BITS - 1 - i)
        return jnp.where(enough(lax.bitcast_convert_type(cand, F32)), cand, thr)

    thr = lax.fori_loop(0, F32_MAGNITUDE_BITS, bit_step, jnp.zeros((rows, 1), I32))
    lo = lax.bitcast_convert_type(thr, F32)
    hi = lax.bitcast_convert_type(thr + 1, F32)

    def refine(i, c):
        lo, hi = c
        mid = lo + (hi - lo) * 0.5
        ok = enough(mid)
        return jnp.where(ok, mid, lo), jnp.where(ok, hi, mid)

    lo, hi = lax.fori_loop(0, REFINE_STEPS, refine, (lo, hi))
    gt = jnp.where(a >= hi, 1.0, 0.0) * jnp.where(a > lo, 1.0, 0.0)
    eq = jnp.where(a >= lo, 1.0, 0.0) - gt
    need = capf - jnp.sum(gt, axis=1, keepdims=True)
    tie_rank = _excl_prefix(eq, t)
    m = gt + eq * jnp.where(tie_rank < need, 1.0, 0.0)
    slot = _excl_prefix(m, t)
    sel = jnp.where(m > 0.5, slot.astype(I32), -1)
    sel_ref[...] = sel
    filler = jnp.full((LANES - N_EXPERTS, t), -1, I32)
    for b in range(bsz):
        selt_ref[b * t:(b + 1) * t, :] = jnp.concatenate(
            [sel[b * N_EXPERTS:(b + 1) * N_EXPERTS, :], filler], axis=0).T


def _route(aff, bsz, t, cap):
    rows = bsz * N_EXPERTS
    sel, selt, afft = pl.pallas_call(
        functools.partial(_route_kernel, bsz=bsz, t=t, cap=cap),
        grid=(1,),
        in_specs=[pl.BlockSpec((bsz * t, LANES), lambda i: (0, 0))],
        out_specs=[
            pl.BlockSpec((rows, t), lambda i: (0, 0)),
            pl.BlockSpec((bsz * t, LANES), lambda i: (0, 0)),
            pl.BlockSpec((rows, t), lambda i: (0, 0)),
        ],
        out_shape=[
            jax.ShapeDtypeStruct((rows, t), I32),
            jax.ShapeDtypeStruct((bsz * t, LANES), I32),
            jax.ShapeDtypeStruct((rows, t), F32),
        ],
        scratch_shapes=[pltpu.VMEM((rows, t), F32)],
        compiler_params=_cparams(),
        name="route",
    )(aff)
    return sel.reshape(bsz, N_EXPERTS, t), selt, afft.reshape(bsz, N_EXPERTS, t)


def _gather_kernel(h_ref, sel_ref, afft_ref, xs_ref, gate_ref, *, t, cap, eg):
    g = pl.program_id(1)
    hb = h_ref[...]
    slot = lax.broadcasted_iota(I32, (cap, t), 0)
    for el in range(eg):
        if eg == N_EXPERTS:
            srow = sel_ref[0, el:el + 1, :]
            arow = afft_ref[0, el:el + 1, :]
        else:
            e = g * eg + el
            srow = sel_ref[0, pl.ds(e, 1), :]
            arow = afft_ref[0, pl.ds(e, 1), :]
        hit = srow == slot
        p = jnp.where(hit, 1.0, 0.0).astype(BF16)
        xs_ref[el, 0] = _dot(p, hb).astype(BF16)
        gate_ref[el, 0] = jnp.sum(jnp.where(hit, arow, 0.0), axis=1, keepdims=True)


GATHER_SLOTS = 64


def _gather_win_kernel(h_ref, sel_ref, afft_ref, xs_ref, gate_ref, p_s, *, t, cap):
    nsub = cap // GATHER_SLOTS
    per = t // nsub
    win = 2 * per
    j = pl.program_id(1)
    s0 = j * GATHER_SLOTS
    lo = pl.multiple_of(jnp.clip(j * per - per // 2, 0, t - win), per // 2)
    sel = sel_ref[0]
    rel = sel - s0
    tok = lax.broadcasted_iota(I32, (1, t), 1)
    mine = jnp.where(rel < 0, 0.0, jnp.where(rel >= GATHER_SLOTS, 0.0, 1.0))
    outside = mine * jnp.where(tok < lo, 1.0, jnp.where(tok >= lo + win, 1.0, 0.0))
    windowed = jnp.max(outside) == 0.0
    slot = lax.broadcasted_iota(I32, (GATHER_SLOTS, 1), 0)

    def run(width, sel_w, aff_w, h_w):
        for e in range(N_EXPERTS):
            hit = (sel_w[e:e + 1, :] - s0) == slot
            p_s[e * GATHER_SLOTS:(e + 1) * GATHER_SLOTS, 0:width] = jnp.where(hit, 1.0, 0.0).astype(BF16)
            gate_ref[e, 0] = jnp.sum(jnp.where(hit, aff_w[e:e + 1, :], 0.0), axis=1, keepdims=True)
        xs = _dot(p_s[:, 0:width], h_w).astype(BF16)
        xs_ref[:, 0] = xs.reshape(N_EXPERTS, GATHER_SLOTS, D_MODEL)

    @pl.when(windowed)
    def _():
        run(win, sel_ref[0, :, pl.ds(lo, win)], afft_ref[0, :, pl.ds(lo, win)], h_ref[pl.ds(lo, win), :])

    @pl.when(jnp.logical_not(windowed))
    def _():
        run(t, sel, afft_ref[0], h_ref[...])


def _gather(h2, sel, afft, bsz, t, cap):
    if cap % GATHER_SLOTS == 0 and cap // GATHER_SLOTS > 1:
        return pl.pallas_call(
            functools.partial(_gather_win_kernel, t=t, cap=cap),
            grid=(bsz, cap // GATHER_SLOTS),
            in_specs=[
                pl.BlockSpec((t, D_MODEL), lambda b, j: (b, 0)),
                pl.BlockSpec((1, N_EXPERTS, t), lambda b, j: (b, 0, 0)),
                pl.BlockSpec((1, N_EXPERTS, t), lambda b, j: (b, 0, 0)),
            ],
            out_specs=[
                pl.BlockSpec((N_EXPERTS, 1, GATHER_SLOTS, D_MODEL), lambda b, j: (0, b, j, 0)),
                pl.BlockSpec((N_EXPERTS, 1, GATHER_SLOTS, 1), lambda b, j: (0, b, j, 0)),
            ],
            out_shape=[
                jax.ShapeDtypeStruct((N_EXPERTS, bsz, cap, D_MODEL), BF16),
                jax.ShapeDtypeStruct((N_EXPERTS, bsz, cap, 1), F32),
            ],
            scratch_shapes=[pltpu.VMEM((N_EXPERTS * GATHER_SLOTS, t), BF16)],
            compiler_params=_cparams(),
            name="gather",
        )(h2, sel, afft)
    eg = max(1, min(N_EXPERTS, 512 // cap))
    return pl.pallas_call(
        functools.partial(_gather_kernel, t=t, cap=cap, eg=eg),
        grid=(bsz, N_EXPERTS // eg),
        in_specs=[
            pl.BlockSpec((t, D_MODEL), lambda b, g: (b, 0)),
            pl.BlockSpec((1, N_EXPERTS, t), lambda b, g: (b, 0, 0)),
            pl.BlockSpec((1, N_EXPERTS, t), lambda b, g: (b, 0, 0)),
        ],
        out_specs=[
            pl.BlockSpec((eg, 1, cap, D_MODEL), lambda b, g: (g, b, 0, 0)),
            pl.BlockSpec((eg, 1, cap, 1), lambda b, g: (g, b, 0, 0)),
        ],
        out_shape=[
            jax.ShapeDtypeStruct((N_EXPERTS, bsz, cap, D_MODEL), BF16),
            jax.ShapeDtypeStruct((N_EXPERTS, bsz, cap, 1), F32),
        ],
        compiler_params=_cparams(),
        name="gather",
    )(h2, sel, afft)


FFN_TF = 256
FFN_TM = 512


def _ffn_kernel(xc_ref, xl_ref, gc_ref, gl_ref, wg_ref, wu_ref, wd_ref, yc_ref, yl_ref, acc_ref):
    def step(first):
        wg = wg_ref[0, 0].astype(BF16)
        wu = wu_ref[0, 0].astype(BF16)
        wd = wd_ref[0, 0].astype(BF16)
        base = 0
        for x_ref, g_ref, y_ref in ((xc_ref, gc_ref, yc_ref), (xl_ref, gl_ref, yl_ref)):
            rows = x_ref.shape[1]
            for r0 in range(0, rows, FFN_TM):
                x = x_ref[0, r0:r0 + FFN_TM, :]
                hm = (_silu(_dot(x, wg)) * _dot(x, wu)).astype(BF16)
                part = _dot(hm, wd)
                if first:
                    acc_ref[base + r0:base + r0 + FFN_TM, :] = part
                else:
                    acc = acc_ref[base + r0:base + r0 + FFN_TM, :] + part
                    acc_ref[base + r0:base + r0 + FFN_TM, :] = acc
                    y_ref[0, r0:r0 + FFN_TM, :] = (acc * g_ref[0, r0:r0 + FFN_TM, :]).astype(BF16)
            base += rows

    j = pl.program_id(1)
    pl.when(j == 0)(functools.partial(step, True))
    pl.when(j > 0)(functools.partial(step, False))


def _ffn(xs_c, gates_c, xs_l, gates_l, w_e_gate, w_e_up, w_e_down, layer):
    rc = xs_c.shape[1]
    rl = xs_l.shape[1]
    assert rc % FFN_TM == 0 and rl % FFN_TM == 0
    rowblk = lambda r, w: pl.BlockSpec((1, r, w), lambda e, j: (e, 0, 0))
    return pl.pallas_call(
        _ffn_kernel,
        grid=(N_EXPERTS, D_EXPERT // FFN_TF),
        in_specs=[
            rowblk(rc, D_MODEL), rowblk(rl, D_MODEL), rowblk(rc, 1), rowblk(rl, 1),
            pl.BlockSpec((1, 1, D_MODEL, FFN_TF), lambda e, j: (layer, e, 0, j)),
            pl.BlockSpec((1, 1, D_MODEL, FFN_TF), lambda e, j: (layer, e, 0, j)),
            pl.BlockSpec((1, 1, FFN_TF, D_MODEL), lambda e, j: (layer, e, j, 0)),
        ],
        out_specs=[rowblk(rc, D_MODEL), rowblk(rl, D_MODEL)],
        out_shape=[
            jax.ShapeDtypeStruct((N_EXPERTS, rc, D_MODEL), BF16),
            jax.ShapeDtypeStruct((N_EXPERTS, rl, D_MODEL), BF16),
        ],
        scratch_shapes=[pltpu.VMEM((rc + rl, D_MODEL), F32)],
        compiler_params=_cparams(),
        name="ffn",
    )(xs_c, xs_l, gates_c, gates_l, w_e_gate, w_e_up, w_e_down)


def _scatter_kernel(*refs, cap, nt, final):
    if final:
        selt_ref, y_ref, x_ref, mod_ref, gf_ref, xo_ref, pt_s, yw_s = refs
    else:
        selt_ref, y_ref, x_ref, mod_ref, xo_ref, pt_s, yw_s = refs
    tt = selt_ref.shape[0]
    ncol = N_EXPERTS * cap
    gw = min(ncol, 1024)
    gt2 = mod_ref[0][:, 5 * D_MODEL:6 * D_MODEL]

    def finish(moe):
        x = x_ref[...] + gt2 * moe
        if final:
            x = _rms(x) * gf_ref[...]
        xo_ref[...] = x

    if nt > 1:
        win = 2 * cap // nt
        st = selt_ref[...]
        first = jnp.min(jnp.where(st < 0, cap, st), axis=0, keepdims=True)
        lo = jnp.minimum(first & -BF16_ROWS, cap - win)
        rel = st - lo
        outside = jnp.where(st < 0, 0.0, jnp.where(rel >= win, 1.0, 0.0))
        windowed = jnp.max(outside) == 0.0

        @pl.when(windowed)
        def _():
            slot = lax.broadcasted_iota(I32, (1, win), 1)
            for e in range(N_EXPERTS):
                pt_s[:, e * win:(e + 1) * win] = jnp.where(rel[:, e:e + 1] == slot, 1.0, 0.0).astype(BF16)
                lo_e = pl.multiple_of(lo[0, e], BF16_ROWS)
                yw_s[e * win:(e + 1) * win, :] = y_ref[e, 0, pl.ds(lo_e, win), :]
            finish(_dot(pt_s[:, 0:N_EXPERTS * win], yw_s[...]))

        @pl.when(jnp.logical_not(windowed))
        def _():
            slot = lax.broadcasted_iota(I32, (1, cap), 1)
            for e in range(N_EXPERTS):
                pt_s[:, e * cap:(e + 1) * cap] = jnp.where(st[:, e:e + 1] == slot, 1.0, 0.0).astype(BF16)
            finish(_dot(pt_s[...], y_ref[:, 0].reshape(ncol, D_MODEL)))

        return
    if cap % LANES == 0:
        st = selt_ref[...]
        slot = lax.broadcasted_iota(I32, (1, cap), 1)
        for e in range(N_EXPERTS):
            pt_s[:, e * cap:(e + 1) * cap] = jnp.where(st[:, e:e + 1] == slot, 1.0, 0.0).astype(BF16)
    else:
        stf = selt_ref[...].astype(F32).astype(BF16)
        col = lax.broadcasted_iota(I32, (LANES, gw), 1)
        row = lax.broadcasted_iota(I32, (LANES, gw), 0)
        cmod = (lax.broadcasted_iota(I32, (1, gw), 1) % cap).astype(F32)
        for g in range(ncol // gw):
            expand = jnp.where((col + g * gw) // cap == row, 1.0, 0.0).astype(BF16)
            selx = _dot(stf, expand)
            pt_s[:, g * gw:(g + 1) * gw] = jnp.where(selx == cmod, 1.0, 0.0).astype(BF16)
    finish(_dot(pt_s[...], y_ref[:, 0].reshape(ncol, D_MODEL)))


def _scatter(selt, y, x, modp, norm_final_g, bsz, t, cap):
    final = norm_final_g is not None
    tt = min(t, 512)
    nt = t // tt
    row = lambda b, i: (b * nt + i, 0)
    in_specs = [
        pl.BlockSpec((tt, LANES), row),
        pl.BlockSpec((N_EXPERTS, 1, cap, D_MODEL), lambda b, i: (0, b, 0, 0)),
        pl.BlockSpec((tt, D_MODEL), row),
        pl.BlockSpec((1, 1, 6 * D_MODEL), lambda b, i: (b, 0, 0)),
    ]
    args = [selt, y, x, modp]
    if final:
        in_specs.append(pl.BlockSpec((1, D_MODEL), lambda b, i: (0, 0)))
        args.append(norm_final_g)
    return pl.pallas_call(
        functools.partial(_scatter_kernel, cap=cap, nt=nt, final=final),
        grid=(bsz, nt),
        in_specs=in_specs,
        out_specs=pl.BlockSpec((tt, D_MODEL), row),
        out_shape=jax.ShapeDtypeStruct((bsz * t, D_MODEL), F32),
        scratch_shapes=[
            pltpu.VMEM((tt, N_EXPERTS * cap), BF16),
            pltpu.VMEM((N_EXPERTS * 2 * cap // nt, D_MODEL) if nt > 1 else (2 * SUBLANES, LANES), BF16),
        ],
        compiler_params=_cparams(),
        name="scatter",
    )(*args)


def _mixer_and_route(x, modp, s0f, s0b, wts, layer, bsz, t, rlen, emit_state):
    cap = EC_FACTOR * t // N_EXPERTS
    proj, u = _inproj_conv(x, modp, wts["norm_mix_g"], wts["w_in"], wts["conv_w"], wts["conv_b"],
                           wts["conv_ln_g"], wts["conv_ln_b"], layer, bsz, t, rlen)
    gla = _gla(proj, wts["wdf"], wts["bdf"], wts["wdb"], wts["bdb"], s0f, s0b, layer, bsz, t, emit_state)
    if emit_state:
        o, sf, sb = gla
    else:
        (o,) = gla
        sf = sb = None
    x, h2, aff = _outproj(o, proj, u, x, modp, wts["gla_norm_g"], wts["w_out"], wts["norm_ffn_g"],
                          wts["w_router"], layer, bsz, t)
    sel, selt, afft = _route(aff, bsz, t, cap)
    xs, gates = _gather(h2, sel, afft, bsz, t, cap)
    return (x, selt, xs.reshape(N_EXPERTS, bsz * cap, D_MODEL), gates.reshape(N_EXPERTS, bsz * cap, 1)), sf, sb


def kernel(x_prompt, x_sample, state_gla_fwd, state_gla_bwd, c, c_ctx, norm_mix_g, norm_ffn_g, norm_final_g, w_mod, b_mod, w_in, w_decay_f, b_decay_f, w_decay_b, b_decay_b, gla_norm_g, conv_w, conv_b, conv_ln_g, conv_ln_b, w_out, w_router, w_e_gate, w_e_up, w_e_down):
    depth = w_in.shape[0]
    bp, tp, _ = x_prompt.shape
    bs, ts, _ = x_sample.shape

    zcols = 2 * GLA_KEY + 2 * MIX_GLA
    w_in_r = jnp.concatenate(
        [w_in[:, :, :zcols + 2 * GATE_RANK], jnp.zeros((depth, D_MODEL, LANES - 2 * GATE_RANK), F32),
         w_in[:, :, zcols + 2 * GATE_RANK:]], axis=2).astype(BF16)
    zpad_f = jnp.zeros((depth, LANES - GATE_RANK, GLA_KEY), F32)
    zpad_b0 = jnp.zeros((depth, GATE_RANK, GLA_KEY), F32)
    zpad_b1 = jnp.zeros((depth, LANES - 2 * GATE_RANK, GLA_KEY), F32)
    vec = lambda a: a.reshape(depth, 1, a.shape[-1])
    wts = dict(
        norm_mix_g=vec(norm_mix_g), norm_ffn_g=vec(norm_ffn_g), w_in=w_in_r,
        wdf=jnp.concatenate([w_decay_f, zpad_f], axis=1).astype(BF16), bdf=vec(b_decay_f),
        wdb=jnp.concatenate([zpad_b0, w_decay_b, zpad_b1], axis=1).astype(BF16), bdb=vec(b_decay_b),
        gla_norm_g=vec(gla_norm_g),
        conv_w=jnp.broadcast_to(
            conv_w.reshape(depth, CONV_W, 1, MIX_CONV // LANES, LANES).transpose(0, 3, 1, 2, 4),
            (depth, MIX_CONV // LANES, CONV_W, SUBLANES, LANES)),
        conv_b=jnp.broadcast_to(conv_b.reshape(depth, MIX_CONV // LANES, 1, LANES),
                                (depth, MIX_CONV // LANES, SUBLANES, LANES)),
        conv_ln_g=vec(conv_ln_g), conv_ln_b=vec(conv_ln_b), w_out=w_out.astype(BF16),
        w_router=jnp.concatenate(
            [w_router, jnp.zeros((depth, D_MODEL, LANES - N_EXPERTS), F32)], axis=2).astype(BF16),
        w_e_gate=w_e_gate, w_e_up=w_e_up, w_e_down=w_e_down,
    )

    nrow = 2 * SUBLANES
    cond = jnp.concatenate([c, c_ctx[None, :], jnp.zeros((nrow - bs - 1, D_MODEL), F32)], axis=0)
    mod = _modulation(cond, w_mod, b_mod)

    xp = x_prompt.reshape(bp * tp, D_MODEL)
    xs = x_sample.reshape(bs * ts, D_MODEL)
    final_g = norm_final_g.reshape(1, D_MODEL)
    fwd_states = []
    bwd_states = []
    for l in range(depth):
        last = l == depth - 1
        mod_ctx = jnp.broadcast_to(mod[l, bs][None, None, :], (bp, 1, 6 * D_MODEL))
        mod_lat = mod[l, :bs][:, None, :]
        (xp, selt_c, xs_c, gates_c), sf, sb = _mixer_and_route(
            xp, mod_ctx, None, None, wts, l, bp, tp, tp, True)
        fwd_states.append(sf)
        bwd_states.append(sb)
        (xs, selt_l, xs_l, gates_l), _, _ = _mixer_and_route(
            xs, mod_lat, state_gla_fwd, state_gla_bwd, wts, l, bs, ts, GRID_W, False)
        y_c, y_l = _ffn(xs_c, gates_c, xs_l, gates_l, wts["w_e_gate"], wts["w_e_up"], wts["w_e_down"], l)
        cap_c = EC_FACTOR * tp // N_EXPERTS
        cap_l = EC_FACTOR * ts // N_EXPERTS
        fin = final_g if last else None
        xp = _scatter(selt_c, y_c.reshape(N_EXPERTS, bp, cap_c, D_MODEL), xp, mod_ctx, fin, bp, tp, cap_c)
        xs = _scatter(selt_l, y_l.reshape(N_EXPERTS, bs, cap_l, D_MODEL), xs, mod_lat, fin, bs, ts, cap_l)
    y_prompt = xp.reshape(bp, tp, D_MODEL)
    y_sample = xs.reshape(bs, ts, D_MODEL)
    return (y_prompt, y_sample, jnp.concatenate(fwd_states, axis=1), jnp.concatenate(bwd_states, axis=1))
```

```python
import functools

import jax
import jax.numpy as jnp
from jax import lax
from jax.experimental import pallas as pl
from jax.experimental.pallas import tpu as pltpu

F32 = jnp.float32
BF16 = jnp.bfloat16
I32 = jnp.int32

D_MODEL = 1024
GRID_W = 64
MIX_GLA = 512
MIX_CONV = 512
GLA_HEADS = 4
GLA_DK = 64
GLA_DV = 128
GLA_KEY = GLA_HEADS * GLA_DK
GATE_RANK = 16
GATE_TAU = 16.0
CHUNK = 64
CONV_W = 31
CONV_HALF = CONV_W // 2
N_EXPERTS = 16
EC_FACTOR = 2
D_EXPERT = 1024
EPS = 1e-6

LANES = 128
SUBLANES = 8

PROJ_KEEP = 2 * GLA_KEY + 2 * MIX_GLA + LANES
PROJ_COLS = PROJ_KEEP + 2 * MIX_CONV
COLBLK = 512
Z_COLBLK = (2 * GLA_KEY + 2 * MIX_GLA) // LANES

VMEM_LIMIT = 56 * 1024 * 1024
CONV_PAD = 16
CONV_CH = 64
OUTPROJ_SUB = 128
GLA_GROUP = 8


def _cparams():
    return pltpu.CompilerParams(vmem_limit_bytes=VMEM_LIMIT)


def _silu(x):
    return x * jax.nn.sigmoid(x)


def _rms(x):
    return x * lax.rsqrt(jnp.mean(x * x, axis=-1, keepdims=True) + EPS)


def _dot(a, b):
    return jnp.dot(a, b, preferred_element_type=F32)


def _mod_kernel(c_ref, w_ref, b_ref, o_ref):
    s = _silu(c_ref[...]).astype(BF16)
    o_ref[0] = _dot(s, w_ref[0].astype(BF16)) + b_ref[0]


def _modulation(cond, w_mod, b_mod):
    depth = w_mod.shape[0]
    n_out = w_mod.shape[2]
    tn = 1536
    rows = cond.shape[0]
    return pl.pallas_call(
        _mod_kernel,
        grid=(depth, n_out // tn),
        in_specs=[
            pl.BlockSpec((rows, D_MODEL), lambda l, j: (0, 0)),
            pl.BlockSpec((1, D_MODEL, tn), lambda l, j: (l, 0, j)),
            pl.BlockSpec((1, 1, tn), lambda l, j: (l, 0, j)),
        ],
        out_specs=pl.BlockSpec((1, rows, tn), lambda l, j: (l, 0, j)),
        out_shape=jax.ShapeDtypeStruct((depth, rows, n_out), F32),
        compiler_params=_cparams(),
        name="modulation",
    )(cond, w_mod, b_mod.reshape(depth, 1, n_out))


def _inproj_conv_kernel(x_ref, mod_ref, g_ref, w_ref, cw_ref, cb_ref, lg_ref, lb_ref,
                        proj_ref, u_ref, pad_ref, y_ref, *, nrows, rlen):
    stride = rlen + 2 * CONV_PAD
    ntile = MIX_CONV // LANES
    tb = nrows * rlen

    @pl.when(pl.program_id(0) == 0)
    def _():
        pad_ref[...] = jnp.zeros(pad_ref.shape, F32)

    assert rlen == CONV_CH or nrows == 1
    pstep = stride if rlen == CONV_CH else CONV_CH
    off = CONV_PAD - CONV_HALF
    ngrp = CONV_CH // SUBLANES
    nshift = (CONV_CH + 2 * CONV_PAD) // SUBLANES - 1
    for c in range(tb // CONV_CH):
        for lt in range(ntile):
            win = pad_ref[lt, c * pstep:c * pstep + CONV_CH + 2 * CONV_PAD, :]
            acc = None
            for s in range(SUBLANES):
                xs = win if s == 0 else win[s:s + nshift * SUBLANES, :]
                xs = xs.reshape(-1, SUBLANES, LANES)
                for a in range(2 * CONV_PAD // SUBLANES):
                    j = SUBLANES * a + s - off
                    if 0 <= j < CONV_W:
                        term = xs[a:a + ngrp] * cw_ref[0, lt, j][None]
                        acc = term if acc is None else acc + term
            y_ref[lt, c * ngrp:(c + 1) * ngrp] = acc + cb_ref[0, lt][None]
    y = jnp.concatenate([y_ref[lt].reshape(tb, LANES) for lt in range(ntile)], axis=1)
    yc = y - jnp.mean(y, axis=-1, keepdims=True)
    yn = yc * lax.rsqrt(jnp.mean(yc * yc, axis=-1, keepdims=True) + EPS)
    u_ref[...] = _silu(yn * lg_ref[0] + lb_ref[0]).astype(BF16)

    m = mod_ref[0]
    sh = m[:, 0:D_MODEL]
    sc = m[:, D_MODEL:2 * D_MODEL]
    h = (_rms(x_ref[...]) * g_ref[0]) * (1.0 + sc) + sh
    p = _dot(h.astype(BF16), w_ref[0])
    proj_ref[...] = p[:, 0:PROJ_KEEP]
    glu = p[:, PROJ_KEEP:PROJ_KEEP + MIX_CONV] * jax.nn.sigmoid(p[:, PROJ_KEEP + MIX_CONV:PROJ_COLS])
    for lt in range(ntile):
        for r in range(nrows):
            base = r * stride + CONV_PAD
            pad_ref[lt, base:base + rlen, :] = glu[r * rlen:(r + 1) * rlen, lt * LANES:(lt + 1) * LANES]


def _inproj_conv(x, modp, norm_g, w_in_b, conv_w, conv_b, ln_g, ln_b, layer, bsz, t, rlen):
    tb = max(rlen, min(t, 512))
    nrows = tb // rlen
    nt = t // tb
    last = bsz * nt - 1
    ntile = MIX_CONV // LANES
    cur = lambda s: jnp.minimum(s, last)
    prev = lambda s: jnp.maximum(s - 1, 0)
    vec = pl.BlockSpec((1, 1, MIX_CONV), lambda s: (layer, 0, 0))
    return pl.pallas_call(
        functools.partial(_inproj_conv_kernel, nrows=nrows, rlen=rlen),
        grid=(bsz * nt + 1,),
        in_specs=[
            pl.BlockSpec((tb, D_MODEL), lambda s: (cur(s), 0)),
            pl.BlockSpec((1, 1, 6 * D_MODEL), lambda s: (cur(s) // nt, 0, 0)),
            pl.BlockSpec((1, 1, D_MODEL), lambda s: (layer, 0, 0)),
            pl.BlockSpec((1, D_MODEL, PROJ_COLS), lambda s: (layer, 0, 0)),
            pl.BlockSpec((1, ntile, CONV_W, SUBLANES, LANES), lambda s: (layer, 0, 0, 0, 0)),
            pl.BlockSpec((1, ntile, SUBLANES, LANES), lambda s: (layer, 0, 0, 0)),
            vec, vec,
        ],
        out_specs=[
            pl.BlockSpec((tb, PROJ_KEEP), lambda s: (cur(s), 0)),
            pl.BlockSpec((tb, MIX_CONV), lambda s: (prev(s), 0)),
        ],
        out_shape=[
            jax.ShapeDtypeStruct((bsz * t, PROJ_KEEP), F32),
            jax.ShapeDtypeStruct((bsz * t, MIX_CONV), BF16),
        ],
        scratch_shapes=[
            pltpu.VMEM((ntile, nrows * (rlen + 2 * CONV_PAD), LANES), F32),
            pltpu.VMEM((ntile, tb // SUBLANES, SUBLANES, LANES), F32),
        ],
        compiler_params=_cparams(),
        name="inproj_conv",
    )(x, modp, norm_g, w_in_b, conv_w, conv_b, ln_g, ln_b)


def _log_sigmoid(x):
    return jnp.minimum(x, 0.0) - jnp.log(1.0 + jnp.exp(-jnp.abs(x)))


def _head_stack(x, lane_head):
    return jnp.concatenate([jnp.where(lane_head == h, x, 0.0) for h in range(GLA_HEADS)], axis=0).astype(BF16)


def _gla_group(chains, lane_head, s_ref, rhs_ref):
    cums = []
    for qk, v, g, tri, causal, d, last_row in chains:
        g_hi = g.astype(BF16)
        g_lo = (g - g_hi.astype(F32)).astype(BF16)
        cums.append(_dot(tri, g_hi) + _dot(tri, g_lo))
    pre = []
    for (qk, v, g, tri, causal, d, last_row), cum in zip(chains, cums):
        cl = cum[last_row:last_row + 1, :]
        q = qk[:, 0:GLA_KEY]
        k = qk[:, GLA_KEY:2 * GLA_KEY]
        qd = (q * (jnp.exp(cum) * (GLA_DK ** -0.5))).astype(BF16)
        k_inv = k * jnp.exp(-cum)
        k_end = k_inv * jnp.exp(cl)
        tr = jnp.concatenate([k_end, cum], axis=0).T
        pre.append((qd, _head_stack(k_inv, lane_head), tr, v.astype(BF16)))
    mm = []
    for (qk, v, g, tri, causal, d, last_row), (qd, ks, tr, vb) in zip(chains, pre):
        sc = lax.dot_general(qd, ks, (((1,), (1,)), ((), ())), preferred_element_type=F32)
        upd = [_dot(tr[h * GLA_DK:(h + 1) * GLA_DK, 0:CHUNK].astype(BF16),
                    vb[:, h * GLA_DV:(h + 1) * GLA_DV]) for h in range(GLA_HEADS)]
        mm.append((sc, jnp.concatenate(upd, axis=0)))
    state = {}
    lhs = []
    for i, ((qk, v, g, tri, causal, d, last_row), (qd, ks, tr, vb), (sc, upd)) in enumerate(zip(chains, pre, mm)):
        s_prev = state[d] if d in state else s_ref[d]
        s_prev_b = s_prev.astype(BF16)
        for h in range(GLA_HEADS):
            rows = slice(h * GLA_DK, (h + 1) * GLA_DK)
            cols = slice(h * GLA_DV, (h + 1) * GLA_DV)
            rhs_ref[i, rows, cols] = vb[:, cols]
            rhs_ref[i, GLA_KEY + h * GLA_DK:GLA_KEY + (h + 1) * GLA_DK, cols] = s_prev_b[rows, :]
        decay = jnp.exp(tr[:, CHUNK + last_row:CHUNK + last_row + 1])
        state[d] = decay * s_prev + upd
        lhs.append(jnp.concatenate([(sc * causal).astype(BF16), qd], axis=1))
    outs = [_dot(l, rhs_ref[i]) for i, l in enumerate(lhs)]
    for d, s_new in state.items():
        s_ref[d] = s_new
    return outs


def _gla_kernel(*refs, t, zero_init, emit_state):
    qk_ref, v_ref, z_ref, wf_ref, bf_ref, wb_ref, bb_ref = refs[:7]
    pos = 7
    if not zero_init:
        s0f_ref, s0b_ref = refs[pos:pos + 2]
        pos += 2
    o_ref = refs[pos]
    pos += 1
    if emit_state:
        sf_ref, sb_ref = refs[pos:pos + 2]
        pos += 2
    gf_s, gb_s, s_ref, mask_s, rhs_s = refs[pos:pos + 5]
    rhs_s[...] = jnp.zeros(rhs_s.shape, BF16)

    z = z_ref[...].astype(BF16)
    gf_s[...] = _log_sigmoid(_dot(z, wf_ref[0]) + bf_ref[0]) * (1.0 / GATE_TAU)
    gb_s[...] = _log_sigmoid(_dot(z, wb_ref[0]) + bb_ref[0]) * (1.0 / GATE_TAU)
    if zero_init:
        s_ref[...] = jnp.zeros(s_ref.shape, F32)
    else:
        s_ref[0] = s0f_ref[0, 0].reshape(GLA_KEY, GLA_DV)
        s_ref[1] = s0b_ref[0, 0].reshape(GLA_KEY, GLA_DV)

    ri = lax.broadcasted_iota(I32, (CHUNK, CHUNK), 0)
    ci = lax.broadcasted_iota(I32, (CHUNK, CHUNK), 1)
    tri_low = jnp.where(ri >= ci, 1.0, 0.0).astype(BF16)
    tri_upp = jnp.where(ri <= ci, 1.0, 0.0).astype(BF16)
    rs = lax.broadcasted_iota(I32, (CHUNK, GLA_HEADS * CHUNK), 0)
    cs = lax.broadcasted_iota(I32, (CHUNK, GLA_HEADS * CHUNK), 1) % CHUNK
    mask_s[0] = jnp.where(rs >= cs, 1.0, 0.0)
    mask_s[1] = jnp.where(rs <= cs, 1.0, 0.0)
    lane_head = lax.broadcasted_iota(I32, (CHUNK, GLA_KEY), 1) // GLA_DK
    n = t // CHUNK

    group = min(GLA_GROUP, n // 2)

    def step(i, first):
        rows = []
        chains = []
        for j in range(group):
            c = i * group + j
            rf = pl.multiple_of(c * CHUNK, CHUNK)
            rb = pl.multiple_of((n - 1 - c) * CHUNK, CHUNK)
            rows += [rf, rb]
            chains.append((qk_ref[pl.ds(rf, CHUNK), :], v_ref[pl.ds(rf, CHUNK), :],
                           gf_s[pl.ds(rf, CHUNK), :], tri_low, mask_s[0], 0, CHUNK - 1))
            chains.append((qk_ref[pl.ds(rb, CHUNK), :], v_ref[pl.ds(rb, CHUNK), :],
                           gb_s[pl.ds(rb, CHUNK), :], tri_upp, mask_s[1], 1, 0))
        outs = _gla_group(chains, lane_head, s_ref, rhs_s)
        for r, o in zip(rows, outs):
            if first:
                o_ref[pl.ds(r, CHUNK), :] = o
            else:
                o_ref[pl.ds(r, CHUNK), :] += o

    def first_half(i, c):
        step(i, True)
        return c

    def second_half(i, c):
        step(i, False)
        return c

    half = n // (2 * group)
    lax.fori_loop(0, half, first_half, 0)
    lax.fori_loop(half, 2 * half, second_half, 0)
    if emit_state:
        sf_ref[0, 0] = s_ref[0].reshape(GLA_HEADS, GLA_DK, GLA_DV)
        sb_ref[0, 0] = s_ref[1].reshape(GLA_HEADS, GLA_DK, GLA_DV)


def _gla(proj, wdf, bdf, wdb, bdb, s0f, s0b, layer, bsz, t, emit_state):
    zero_init = s0f is None
    st_block = (1, 1, GLA_HEADS, GLA_DK, GLA_DV)
    in_specs = [
        pl.BlockSpec((t, COLBLK), lambda b: (b, 0)),
        pl.BlockSpec((t, COLBLK), lambda b: (b, 1)),
        pl.BlockSpec((t, LANES), lambda b: (b, Z_COLBLK)),
        pl.BlockSpec((1, LANES, GLA_KEY), lambda b: (layer, 0, 0)),
        pl.BlockSpec((1, 1, GLA_KEY), lambda b: (layer, 0, 0)),
        pl.BlockSpec((1, LANES, GLA_KEY), lambda b: (layer, 0, 0)),
        pl.BlockSpec((1, 1, GLA_KEY), lambda b: (layer, 0, 0)),
    ]
    args = [proj, proj, proj, wdf, bdf, wdb, bdb]
    if not zero_init:
        in_specs += [pl.BlockSpec(st_block, lambda b: (b, layer, 0, 0, 0))] * 2
        args += [s0f, s0b]
    out_specs = [pl.BlockSpec((t, MIX_GLA), lambda b: (b, 0))]
    out_shape = [jax.ShapeDtypeStruct((bsz * t, MIX_GLA), F32)]
    if emit_state:
        out_specs += [pl.BlockSpec((1, 1, GLA_HEADS, GLA_DK, GLA_DV), lambda b: (b, 0, 0, 0, 0))] * 2
        out_shape += [jax.ShapeDtypeStruct((bsz, 1, GLA_HEADS, GLA_DK, GLA_DV), F32)] * 2
    return pl.pallas_call(
        functools.partial(_gla_kernel, t=t, zero_init=zero_init, emit_state=emit_state),
        grid=(bsz,),
        in_specs=in_specs,
        out_specs=out_specs,
        out_shape=out_shape,
        scratch_shapes=[
            pltpu.VMEM((t, GLA_KEY), F32),
            pltpu.VMEM((t, GLA_KEY), F32),
            pltpu.VMEM((2, GLA_KEY, GLA_DV), F32),
            pltpu.VMEM((2, CHUNK, GLA_HEADS * CHUNK), F32),
            pltpu.VMEM((2 * min(GLA_GROUP, t // CHUNK // 2), 2 * GLA_KEY, MIX_GLA), BF16),
        ],
        compiler_params=_cparams(),
        name="gla",
    )(*args)


def _outproj_kernel(o_ref, r_ref, u_ref, x_ref, mod_ref, gn_ref, wo_ref, g2_ref, wr_ref,
                    xo_ref, h2_ref, aff_ref):
    gn = gn_ref[0]
    m = mod_ref[0]
    gt1 = m[:, 2 * D_MODEL:3 * D_MODEL]
    sh2 = m[:, 3 * D_MODEL:4 * D_MODEL]
    sc2 = m[:, 4 * D_MODEL:5 * D_MODEL]
    tm = o_ref.shape[0]
    subs = [slice(r0, r0 + OUTPROJ_SUB) for r0 in range(0, tm, OUTPROJ_SUB)]
    acts = []
    for rs in subs:
        o = o_ref[rs, :]
        r = r_ref[rs, :]
        parts = []
        for h in range(GLA_HEADS):
            hs = slice(h * GLA_DV, (h + 1) * GLA_DV)
            parts.append(((_rms(o[:, hs]) * gn) * _silu(r[:, hs])).astype(BF16))
        parts.append(u_ref[rs, :])
        acts.append(jnp.concatenate(parts, axis=1))
    mixed = [_dot(a, wo_ref[0]) for a in acts]
    h2s = []
    for rs, mix in zip(subs, mixed):
        x = x_ref[rs, :] + gt1 * mix
        xo_ref[rs, :] = x
        h2 = ((_rms(x) * g2_ref[0]) * (1.0 + sc2) + sh2).astype(BF16)
        h2_ref[rs, :] = h2
        h2s.append(h2)
    logit = [_dot(h2, wr_ref[0]) for h2 in h2s]
    for rs, logits in zip(subs, logit):
        lane = lax.broadcasted_iota(I32, logits.shape, 1)
        logits = jnp.where(lane < N_EXPERTS, logits, -jnp.inf)
        ex = jnp.exp(logits - jnp.max(logits, axis=-1, keepdims=True))
        aff_ref[rs, :] = ex / jnp.sum(ex, axis=-1, keepdims=True)


def _outproj(o, proj, u, x, modp, gla_norm_g, w_out_b, norm_ffn_g, w_router_b, layer, bsz, t):
    tm = min(t, 512)
    nt = t // tm
    row = lambda b, i: (b * nt + i, 0)
    return pl.pallas_call(
        _outproj_kernel,
        grid=(bsz, nt),
        in_specs=[
            pl.BlockSpec((tm, MIX_GLA), row),
            pl.BlockSpec((tm, COLBLK), lambda b, i: (b * nt + i, 2)),
            pl.BlockSpec((tm, MIX_CONV), row),
            pl.BlockSpec((tm, D_MODEL), row),
            pl.BlockSpec((1, 1, 6 * D_MODEL), lambda b, i: (b, 0, 0)),
            pl.BlockSpec((1, 1, GLA_DV), lambda b, i: (layer, 0, 0)),
            pl.BlockSpec((1, D_MODEL, D_MODEL), lambda b, i: (layer, 0, 0)),
            pl.BlockSpec((1, 1, D_MODEL), lambda b, i: (layer, 0, 0)),
            pl.BlockSpec((1, D_MODEL, LANES), lambda b, i: (layer, 0, 0)),
        ],
        out_specs=[
            pl.BlockSpec((tm, D_MODEL), row),
            pl.BlockSpec((tm, D_MODEL), row),
            pl.BlockSpec((tm, LANES), row),
        ],
        out_shape=[
            jax.ShapeDtypeStruct((bsz * t, D_MODEL), F32),
            jax.ShapeDtypeStruct((bsz * t, D_MODEL), BF16),
            jax.ShapeDtypeStruct((bsz * t, LANES), F32),
        ],
        compiler_params=_cparams(),
        name="outproj",
    )(o, proj, u, x, modp, gla_norm_g, w_out_b, norm_ffn_g, w_router_b)


PREFIX_BLK = 256


def _excl_prefix(x, t):
    blk = min(t, PREFIX_BLK)
    ri = lax.broadcasted_iota(I32, (blk, blk), 0)
    ci = lax.broadcasted_iota(I32, (blk, blk), 1)
    upper = jnp.where(ri < ci, 1.0, 0.0).astype(BF16)
    run = jnp.zeros((x.shape[0], 1), F32)
    outs = []
    for j in range(t // blk):
        xb = x[:, j * blk:(j + 1) * blk]
        outs.append(_dot(xb.astype(BF16), upper) + run)
        run = run + jnp.sum(xb, axis=1, keepdims=True)
    return jnp.concatenate(outs, axis=1) if len(outs) > 1 else outs[0]


F32_MAGNITUDE_BITS = 31
REFINE_STEPS = 16


def _route_kernel(aff_ref, sel_ref, selt_ref, afft_ref, at_s, *, bsz, t, cap):
    for b in range(bsz):
        at_s[b * N_EXPERTS:(b + 1) * N_EXPERTS, :] = aff_ref[b * t:(b + 1) * t, :].T[0:N_EXPERTS, :]
    a = at_s[...]
    afft_ref[...] = a
    rows = bsz * N_EXPERTS
    capf = float(cap)

    def enough(th):
        return jnp.sum(jnp.where(a >= th, 1.0, 0.0), axis=1, keepdims=True) >= capf

    def bit_step(i, thr):
        cand = thr | jnp.left_shift(jnp.int32(1), F32_MAGNITUDE_BITS - 1 - i)
        return jnp.where(enough(lax.bitcast_convert_type(cand, F32)), cand, thr)

    thr = lax.fori_loop(0, F32_MAGNITUDE_BITS, bit_step, jnp.zeros((rows, 1), I32))
    lo = lax.bitcast_convert_type(thr, F32)
    hi = lax.bitcast_convert_type(thr + 1, F32)

    def refine(i, c):
        lo, hi = c
        mid = lo + (hi - lo) * 0.5
        ok = enough(mid)
        return jnp.where(ok, mid, lo), jnp.where(ok, hi, mid)

    lo, hi = lax.fori_loop(0, REFINE_STEPS, refine, (lo, hi))
    gt = jnp.where(a >= hi, 1.0, 0.0) * jnp.where(a > lo, 1.0, 0.0)
    eq = jnp.where(a >= lo, 1.0, 0.0) - gt
    need = capf - jnp.sum(gt, axis=1, keepdims=True)
    tie_rank = _excl_prefix(eq, t)
    m = gt + eq * jnp.where(tie_rank < need, 1.0, 0.0)
    slot = _excl_prefix(m, t)
    sel = jnp.where(m > 0.5, slot.astype(I32), -1)
    sel_ref[...] = sel
    filler = jnp.full((LANES - N_EXPERTS, t), -1, I32)
    for b in range(bsz):
        selt_ref[b * t:(b + 1) * t, :] = jnp.concatenate(
            [sel[b * N_EXPERTS:(b + 1) * N_EXPERTS, :], filler], axis=0).T


def _route(aff, bsz, t, cap):
    rows = bsz * N_EXPERTS
    sel, selt, afft = pl.pallas_call(
        functools.partial(_route_kernel, bsz=bsz, t=t, cap=cap),
        grid=(1,),
        in_specs=[pl.BlockSpec((bsz * t, LANES), lambda i: (0, 0))],
        out_specs=[
            pl.BlockSpec((rows, t), lambda i: (0, 0)),
            pl.BlockSpec((bsz * t, LANES), lambda i: (0, 0)),
            pl.BlockSpec((rows, t), lambda i: (0, 0)),
        ],
        out_shape=[
            jax.ShapeDtypeStruct((rows, t), I32),
            jax.ShapeDtypeStruct((bsz * t, LANES), I32),
            jax.ShapeDtypeStruct((rows, t), F32),
        ],
        scratch_shapes=[pltpu.VMEM((rows, t), F32)],
        compiler_params=_cparams(),
        name="route",
    )(aff)
    return sel.reshape(bsz, N_EXPERTS, t), selt, afft.reshape(bsz, N_EXPERTS, t)


def _gather_kernel(h_ref, sel_ref, afft_ref, xs_ref, gate_ref, *, t, cap, eg):
    g = pl.program_id(1)
    hb = h_ref[...]
    slot = lax.broadcasted_iota(I32, (cap, t), 0)
    for el in range(eg):
        if eg == N_EXPERTS:
            srow = sel_ref[0, el:el + 1, :]
            arow = afft_ref[0, el:el + 1, :]
        else:
            e = g * eg + el
            srow = sel_ref[0, pl.ds(e, 1), :]
            arow = afft_ref[0, pl.ds(e, 1), :]
        hit = srow == slot
        p = jnp.where(hit, 1.0, 0.0).astype(BF16)
        xs_ref[el, 0] = _dot(p, hb).astype(BF16)
        gate_ref[el, 0] = jnp.sum(jnp.where(hit, arow, 0.0), axis=1, keepdims=True)


GATHER_SLOTS = 64


def _gather_win_kernel(h_ref, sel_ref, afft_ref, xs_ref, gate_ref, p_s, *, t, cap):
    nsub = cap // GATHER_SLOTS
    per = t // nsub
    win = 2 * per
    j = pl.program_id(1)
    s0 = j * GATHER_SLOTS
    lo = pl.multiple_of(jnp.clip(j * per - per // 2, 0, t - win), per // 2)
    sel = sel_ref[0]
    rel = sel - s0
    tok = lax.broadcasted_iota(I32, (1, t), 1)
    mine = jnp.where(rel < 0, 0.0, jnp.where(rel >= GATHER_SLOTS, 0.0, 1.0))
    outside = mine * jnp.where(tok < lo, 1.0, jnp.where(tok >= lo + win, 1.0, 0.0))
    windowed = jnp.max(outside) == 0.0
    slot = lax.broadcasted_iota(I32, (GATHER_SLOTS, 1), 0)

    def run(width, sel_w, aff_w, h_w):
        for e in range(N_EXPERTS):
            hit = (sel_w[e:e + 1, :] - s0) == slot
            p_s[e * GATHER_SLOTS:(e + 1) * GATHER_SLOTS, 0:width] = jnp.where(hit, 1.0, 0.0).astype(BF16)
            gate_ref[e, 0] = jnp.sum(jnp.where(hit, aff_w[e:e + 1, :], 0.0), axis=1, keepdims=True)
        xs = _dot(p_s[:, 0:width], h_w).astype(BF16)
        xs_ref[:, 0] = xs.reshape(N_EXPERTS, GATHER_SLOTS, D_MODEL)

    @pl.when(windowed)
    def _():
        run(win, sel_ref[0, :, pl.ds(lo, win)], afft_ref[0, :, pl.ds(lo, win)], h_ref[pl.ds(lo, win), :])

    @pl.when(jnp.logical_not(windowed))
    def _():
        run(t, sel, afft_ref[0], h_ref[...])


def _gather(h2, sel, afft, bsz, t, cap):
    if cap % GATHER_SLOTS == 0 and cap // GATHER_SLOTS > 1:
        return pl.pallas_call(
            functools.partial(_gather_win_kernel, t=t, cap=cap),
            grid=(bsz, cap // GATHER_SLOTS),
            in_specs=[
                pl.BlockSpec((t, D_MODEL), lambda b, j: (b, 0)),
                pl.BlockSpec((1, N_EXPERTS, t), lambda b, j: (b, 0, 0)),
                pl.BlockSpec((1, N_EXPERTS, t), lambda b, j: (b, 0, 0)),
            ],
            out_specs=[
                pl.BlockSpec((N_EXPERTS, 1, GATHER_SLOTS, D_MODEL), lambda b, j: (0, b, j, 0)),
                pl.BlockSpec((N_EXPERTS, 1, GATHER_SLOTS, 1), lambda b, j: (0, b, j, 0)),
            ],
            out_shape=[
                jax.ShapeDtypeStruct((N_EXPERTS, bsz, cap, D_MODEL), BF16),
                jax.ShapeDtypeStruct((N_EXPERTS, bsz, cap, 1), F32),
            ],
            scratch_shapes=[pltpu.VMEM((N_EXPERTS * GATHER_SLOTS, t), BF16)],
            compiler_params=_cparams(),
            name="gather",
        )(h2, sel, afft)
    eg = max(1, min(N_EXPERTS, 512 // cap))
    return pl.pallas_call(
        functools.partial(_gather_kernel, t=t, cap=cap, eg=eg),
        grid=(bsz, N_EXPERTS // eg),
        in_specs=[
            pl.BlockSpec((t, D_MODEL), lambda b, g: (b, 0)),
            pl.BlockSpec((1, N_EXPERTS, t), lambda b, g: (b, 0, 0)),
            pl.BlockSpec((1, N_EXPERTS, t), lambda b, g: (b, 0, 0)),
        ],
        out_specs=[
            pl.BlockSpec((eg, 1, cap, D_MODEL), lambda b, g: (g, b, 0, 0)),
            pl.BlockSpec((eg, 1, cap, 1), lambda b, g: (g, b, 0, 0)),
        ],
        out_shape=[
            jax.ShapeDtypeStruct((N_EXPERTS, bsz, cap, D_MODEL), BF16),
            jax.ShapeDtypeStruct((N_EXPERTS, bsz, cap, 1), F32),
        ],
        compiler_params=_cparams(),
        name="gather",
    )(h2, sel, afft)


FFN_TF = 512
FFN_TM = 512


def _ffn_kernel(xc_ref, xl_ref, gc_ref, gl_ref, wg_ref, wu_ref, wd_ref, yc_ref, yl_ref, acc_ref):
    def step(first):
        wg = wg_ref[0, 0].astype(BF16)
        wu = wu_ref[0, 0].astype(BF16)
        wd = wd_ref[0, 0].astype(BF16)
        base = 0
        for x_ref, g_ref, y_ref in ((xc_ref, gc_ref, yc_ref), (xl_ref, gl_ref, yl_ref)):
            rows = x_ref.shape[1]
            for r0 in range(0, rows, FFN_TM):
                x = x_ref[0, r0:r0 + FFN_TM, :]
                hm = (_silu(_dot(x, wg)) * _dot(x, wu)).astype(BF16)
                part = _dot(hm, wd)
                if first:
                    acc_ref[base + r0:base + r0 + FFN_TM, :] = part
                else:
                    acc = acc_ref[base + r0:base + r0 + FFN_TM, :] + part
                    acc_ref[base + r0:base + r0 + FFN_TM, :] = acc
                    y_ref[0, r0:r0 + FFN_TM, :] = (acc * g_ref[0, r0:r0 + FFN_TM, :]).astype(BF16)
            base += rows

    j = pl.program_id(1)
    pl.when(j == 0)(functools.partial(step, True))
    pl.when(j > 0)(functools.partial(step, False))


def _ffn(xs_c, gates_c, xs_l, gates_l, w_e_gate, w_e_up, w_e_down, layer):
    rc = xs_c.shape[1]
    rl = xs_l.shape[1]
    assert rc % FFN_TM == 0 and rl % FFN_TM == 0
    rowblk = lambda r, w: pl.BlockSpec((1, r, w), lambda e, j: (e, 0, 0))
    return pl.pallas_call(
        _ffn_kernel,
        grid=(N_EXPERTS, D_EXPERT // FFN_TF),
        in_specs=[
            rowblk(rc, D_MODEL), rowblk(rl, D_MODEL), rowblk(rc, 1), rowblk(rl, 1),
            pl.BlockSpec((1, 1, D_MODEL, FFN_TF), lambda e, j: (layer, e, 0, j)),
            pl.BlockSpec((1, 1, D_MODEL, FFN_TF), lambda e, j: (layer, e, 0, j)),
            pl.BlockSpec((1, 1, FFN_TF, D_MODEL), lambda e, j: (layer, e, j, 0)),
        ],
        out_specs=[rowblk(rc, D_MODEL), rowblk(rl, D_MODEL)],
        out_shape=[
            jax.ShapeDtypeStruct((N_EXPERTS, rc, D_MODEL), BF16),
            jax.ShapeDtypeStruct((N_EXPERTS, rl, D_MODEL), BF16),
        ],
        scratch_shapes=[pltpu.VMEM((rc + rl, D_MODEL), F32)],
        compiler_params=_cparams(),
        name="ffn",
    )(xs_c, xs_l, gates_c, gates_l, w_e_gate, w_e_up, w_e_down)


def _scatter_kernel(*refs, cap, nt, final):
    if final:
        selt_ref, y_ref, x_ref, mod_ref, gf_ref, xo_ref, pt_s, yw_s = refs
    else:
        selt_ref, y_ref, x_ref, mod_ref, xo_ref, pt_s, yw_s = refs
    tt = selt_ref.shape[0]
    ncol = N_EXPERTS * cap
    gw = min(ncol, 1024)
    gt2 = mod_ref[0][:, 5 * D_MODEL:6 * D_MODEL]

    def finish(moe):
        x = x_ref[...] + gt2 * moe
        if final:
            x = _rms(x) * gf_ref[...]
        xo_ref[...] = x

    if nt > 1:
        per = cap // nt
        win = 2 * per
        lo = pl.multiple_of(jnp.clip(pl.program_id(1) * per - per // 2, 0, cap - win), per // 2)
        st = selt_ref[...]
        rel = st - lo
        outside = jnp.where(st < 0, 0.0, jnp.where(rel < 0, 1.0, jnp.where(rel >= win, 1.0, 0.0)))
        windowed = jnp.max(outside) == 0.0

        @pl.when(windowed)
        def _():
            slot = lax.broadcasted_iota(I32, (1, win), 1)
            for e in range(N_EXPERTS):
                pt_s[:, e * win:(e + 1) * win] = jnp.where(rel[:, e:e + 1] == slot, 1.0, 0.0).astype(BF16)
                yw_s[e * win:(e + 1) * win, :] = y_ref[e, 0, pl.ds(lo, win), :]
            finish(_dot(pt_s[:, 0:N_EXPERTS * win], yw_s[...]))

        @pl.when(jnp.logical_not(windowed))
        def _():
            slot = lax.broadcasted_iota(I32, (1, cap), 1)
            for e in range(N_EXPERTS):
                pt_s[:, e * cap:(e + 1) * cap] = jnp.where(st[:, e:e + 1] == slot, 1.0, 0.0).astype(BF16)
            finish(_dot(pt_s[...], y_ref[:, 0].reshape(ncol, D_MODEL)))

        return
    if cap % LANES == 0:
        st = selt_ref[...]
        slot = lax.broadcasted_iota(I32, (1, cap), 1)
        for e in range(N_EXPERTS):
            pt_s[:, e * cap:(e + 1) * cap] = jnp.where(st[:, e:e + 1] == slot, 1.0, 0.0).astype(BF16)
    else:
        stf = selt_ref[...].astype(F32).astype(BF16)
        col = lax.broadcasted_iota(I32, (LANES, gw), 1)
        row = lax.broadcasted_iota(I32, (LANES, gw), 0)
        cmod = (lax.broadcasted_iota(I32, (1, gw), 1) % cap).astype(F32)
        for g in range(ncol // gw):
            expand = jnp.where((col + g * gw) // cap == row, 1.0, 0.0).astype(BF16)
            selx = _dot(stf, expand)
            pt_s[:, g * gw:(g + 1) * gw] = jnp.where(selx == cmod, 1.0, 0.0).astype(BF16)
    finish(_dot(pt_s[...], y_ref[:, 0].reshape(ncol, D_MODEL)))


def _scatter(selt, y, x, modp, norm_final_g, bsz, t, cap):
    final = norm_final_g is not None
    tt = min(t, 512)
    nt = t // tt
    row = lambda b, i: (b * nt + i, 0)
    in_specs = [
        pl.BlockSpec((tt, LANES), row),
        pl.BlockSpec((N_EXPERTS, 1, cap, D_MODEL), lambda b, i: (0, b, 0, 0)),
        pl.BlockSpec((tt, D_MODEL), row),
        pl.BlockSpec((1, 1, 6 * D_MODEL), lambda b, i: (b, 0, 0)),
    ]
    args = [selt, y, x, modp]
    if final:
        in_specs.append(pl.BlockSpec((1, D_MODEL), lambda b, i: (0, 0)))
        args.append(norm_final_g)
    return pl.pallas_call(
        functools.partial(_scatter_kernel, cap=cap, nt=nt, final=final),
        grid=(bsz, nt),
        in_specs=in_specs,
        out_specs=pl.BlockSpec((tt, D_MODEL), row),
        out_shape=jax.ShapeDtypeStruct((bsz * t, D_MODEL), F32),
        scratch_shapes=[
            pltpu.VMEM((tt, N_EXPERTS * cap), BF16),
            pltpu.VMEM((N_EXPERTS * 2 * cap // nt, D_MODEL) if nt > 1 else (2 * SUBLANES, LANES), BF16),
        ],
        compiler_params=_cparams(),
        name="scatter",
    )(*args)


def _mixer_and_route(x, modp, s0f, s0b, wts, layer, bsz, t, rlen, emit_state):
    cap = EC_FACTOR * t // N_EXPERTS
    proj, u = _inproj_conv(x, modp, wts["norm_mix_g"], wts["w_in"], wts["conv_w"], wts["conv_b"],
                           wts["conv_ln_g"], wts["conv_ln_b"], layer, bsz, t, rlen)
    gla = _gla(proj, wts["wdf"], wts["bdf"], wts["wdb"], wts["bdb"], s0f, s0b, layer, bsz, t, emit_state)
    if emit_state:
        o, sf, sb = gla
    else:
        (o,) = gla
        sf = sb = None
    x, h2, aff = _outproj(o, proj, u, x, modp, wts["gla_norm_g"], wts["w_out"], wts["norm_ffn_g"],
                          wts["w_router"], layer, bsz, t)
    sel, selt, afft = _route(aff, bsz, t, cap)
    xs, gates = _gather(h2, sel, afft, bsz, t, cap)
    return (x, selt, xs.reshape(N_EXPERTS, bsz * cap, D_MODEL), gates.reshape(N_EXPERTS, bsz * cap, 1)), sf, sb


def kernel(x_prompt, x_sample, state_gla_fwd, state_gla_bwd, c, c_ctx, norm_mix_g, norm_ffn_g, norm_final_g, w_mod, b_mod, w_in, w_decay_f, b_decay_f, w_decay_b, b_decay_b, gla_norm_g, conv_w, conv_b, conv_ln_g, conv_ln_b, w_out, w_router, w_e_gate, w_e_up, w_e_down):
    depth = w_in.shape[0]
    bp, tp, _ = x_prompt.shape
    bs, ts, _ = x_sample.shape

    zcols = 2 * GLA_KEY + 2 * MIX_GLA
    zend = zcols + 2 * GATE_RANK
    w_in_r = jnp.zeros((depth, D_MODEL, PROJ_COLS), BF16)
    w_in_r = lax.dynamic_update_slice(w_in_r, w_in[:, :, :zend].astype(BF16), (0, 0, 0))
    w_in_r = lax.dynamic_update_slice(w_in_r, w_in[:, :, zend:].astype(BF16), (0, 0, PROJ_KEEP))
    zpad_f = jnp.zeros((depth, LANES - GATE_RANK, GLA_KEY), F32)
    zpad_b0 = jnp.zeros((depth, GATE_RANK, GLA_KEY), F32)
    zpad_b1 = jnp.zeros((depth, LANES - 2 * GATE_RANK, GLA_KEY), F32)
    vec = lambda a: a.reshape(depth, 1, a.shape[-1])
    wts = dict(
        norm_mix_g=vec(norm_mix_g), norm_ffn_g=vec(norm_ffn_g), w_in=w_in_r,
        wdf=jnp.concatenate([w_decay_f, zpad_f], axis=1).astype(BF16), bdf=vec(b_decay_f),
        wdb=jnp.concatenate([zpad_b0, w_decay_b, zpad_b1], axis=1).astype(BF16), bdb=vec(b_decay_b),
        gla_norm_g=vec(gla_norm_g),
        conv_w=jnp.broadcast_to(
            conv_w.reshape(depth, CONV_W, 1, MIX_CONV // LANES, LANES).transpose(0, 3, 1, 2, 4),
            (depth, MIX_CONV // LANES, CONV_W, SUBLANES, LANES)),
        conv_b=jnp.broadcast_to(conv_b.reshape(depth, MIX_CONV // LANES, 1, LANES),
                                (depth, MIX_CONV // LANES, SUBLANES, LANES)),
        conv_ln_g=vec(conv_ln_g), conv_ln_b=vec(conv_ln_b), w_out=w_out.astype(BF16),
        w_router=jnp.concatenate(
            [w_router, jnp.zeros((depth, D_MODEL, LANES - N_EXPERTS), F32)], axis=2).astype(BF16),
        w_e_gate=w_e_gate, w_e_up=w_e_up, w_e_down=w_e_down,
    )

    nrow = 2 * SUBLANES
    cond = jnp.concatenate([c, c_ctx[None, :], jnp.zeros((nrow - bs - 1, D_MODEL), F32)], axis=0)
    mod = _modulation(cond, w_mod, b_mod)

    xp = x_prompt.reshape(bp * tp, D_MODEL)
    xs = x_sample.reshape(bs * ts, D_MODEL)
    final_g = norm_final_g.reshape(1, D_MODEL)
    fwd_states = []
    bwd_states = []
    for l in range(depth):
        last = l == depth - 1
        mod_ctx = jnp.broadcast_to(mod[l, bs][None, None, :], (bp, 1, 6 * D_MODEL))
        mod_lat = mod[l, :bs][:, None, :]
        (xp, selt_c, xs_c, gates_c), sf, sb = _mixer_and_route(
            xp, mod_ctx, None, None, wts, l, bp, tp, tp, True)
        fwd_states.append(sf)
        bwd_states.append(sb)
        (xs, selt_l, xs_l, gates_l), _, _ = _mixer_and_route(
            xs, mod_lat, state_gla_fwd, state_gla_bwd, wts, l, bs, ts, GRID_W, False)
        y_c, y_l = _ffn(xs_c, gates_c, xs_l, gates_l, wts["w_e_gate"], wts["w_e_up"], wts["w_e_down"], l)
        cap_c = EC_FACTOR * tp // N_EXPERTS
        cap_l = EC_FACTOR * ts // N_EXPERTS
        fin = final_g if last else None
        xp = _scatter(selt_c, y_c.reshape(N_EXPERTS, bp, cap_c, D_MODEL), xp, mod_ctx, fin, bp, tp, cap_c)
        xs = _scatter(selt_l, y_l.reshape(N_EXPERTS, bs, cap_l, D_MODEL), xs, mod_lat, fin, bs, ts, cap_l)
    y_prompt = xp.reshape(bp, tp, D_MODEL)
    y_sample = xs.reshape(bs, ts, D_MODEL)
    return (y_prompt, y_sample, jnp.concatenate(fwd_states, axis=1), jnp.concatenate(bwd_states, axis=1))
```

```python
import functools

import jax
import jax.numpy as jnp
from jax import lax
from jax.experimental import pallas as pl
from jax.experimental.pallas import tpu as pltpu

F32 = jnp.float32
BF16 = jnp.bfloat16
I32 = jnp.int32

D_MODEL = 1024
GRID_W = 64
MIX_GLA = 512
MIX_CONV = 512
GLA_HEADS = 4
GLA_DK = 64
GLA_DV = 128
GLA_KEY = GLA_HEADS * GLA_DK
GATE_RANK = 16
GATE_TAU = 16.0
CHUNK = 64
CONV_W = 31
CONV_HALF = CONV_W // 2
N_EXPERTS = 16
EC_FACTOR = 2
D_EXPERT = 1024
EPS = 1e-6

LANES = 128
SUBLANES = 8
BF16_ROWS = 16

PROJ_KEEP = 2 * GLA_KEY + 2 * MIX_GLA + LANES
PROJ_COLS = PROJ_KEEP + 2 * MIX_CONV
COLBLK = 512
Z_COLBLK = (2 * GLA_KEY + 2 * MIX_GLA) // LANES

VMEM_LIMIT = 56 * 1024 * 1024
CONV_PAD = 16
CONV_CH = 64
OUTPROJ_SUB = 128
GLA_GROUP = 8


def _cparams():
    return pltpu.CompilerParams(vmem_limit_bytes=VMEM_LIMIT)


def _silu(x):
    return x * jax.nn.sigmoid(x)


def _rms(x):
    return x * lax.rsqrt(jnp.mean(x * x, axis=-1, keepdims=True) + EPS)


def _dot(a, b):
    return jnp.dot(a, b, preferred_element_type=F32)


def _mod_kernel(c_ref, w_ref, b_ref, o_ref):
    s = _silu(c_ref[...]).astype(BF16)
    o_ref[0] = _dot(s, w_ref[0].astype(BF16)) + b_ref[0]


def _modulation(cond, w_mod, b_mod):
    depth = w_mod.shape[0]
    n_out = w_mod.shape[2]
    tn = 1536
    rows = cond.shape[0]
    return pl.pallas_call(
        _mod_kernel,
        grid=(depth, n_out // tn),
        in_specs=[
            pl.BlockSpec((rows, D_MODEL), lambda l, j: (0, 0)),
            pl.BlockSpec((1, D_MODEL, tn), lambda l, j: (l, 0, j)),
            pl.BlockSpec((1, 1, tn), lambda l, j: (l, 0, j)),
        ],
        out_specs=pl.BlockSpec((1, rows, tn), lambda l, j: (l, 0, j)),
        out_shape=jax.ShapeDtypeStruct((depth, rows, n_out), F32),
        compiler_params=_cparams(),
        name="modulation",
    )(cond, w_mod, b_mod.reshape(depth, 1, n_out))


def _inproj_conv_kernel(x_ref, mod_ref, g_ref, w_ref, cw_ref, cb_ref, lg_ref, lb_ref,
                        proj_ref, u_ref, pad_ref, y_ref, *, nrows, rlen):
    stride = rlen + 2 * CONV_PAD
    ntile = MIX_CONV // LANES
    tb = nrows * rlen

    @pl.when(pl.program_id(0) == 0)
    def _():
        pad_ref[...] = jnp.zeros(pad_ref.shape, F32)

    assert rlen == CONV_CH or nrows == 1
    pstep = stride if rlen == CONV_CH else CONV_CH
    off = CONV_PAD - CONV_HALF
    ngrp = CONV_CH // SUBLANES
    nshift = (CONV_CH + 2 * CONV_PAD) // SUBLANES - 1
    for c in range(tb // CONV_CH):
        for lt in range(ntile):
            win = pad_ref[lt, c * pstep:c * pstep + CONV_CH + 2 * CONV_PAD, :]
            acc = None
            for s in range(SUBLANES):
                xs = win if s == 0 else win[s:s + nshift * SUBLANES, :]
                xs = xs.reshape(-1, SUBLANES, LANES)
                for a in range(2 * CONV_PAD // SUBLANES):
                    j = SUBLANES * a + s - off
                    if 0 <= j < CONV_W:
                        term = xs[a:a + ngrp] * cw_ref[0, lt, j][None]
                        acc = term if acc is None else acc + term
            y_ref[lt, c * ngrp:(c + 1) * ngrp] = acc + cb_ref[0, lt][None]
    y = jnp.concatenate([y_ref[lt].reshape(tb, LANES) for lt in range(ntile)], axis=1)
    yc = y - jnp.mean(y, axis=-1, keepdims=True)
    yn = yc * lax.rsqrt(jnp.mean(yc * yc, axis=-1, keepdims=True) + EPS)
    u_ref[...] = _silu(yn * lg_ref[0] + lb_ref[0]).astype(BF16)

    m = mod_ref[0]
    sh = m[:, 0:D_MODEL]
    sc = m[:, D_MODEL:2 * D_MODEL]
    h = (_rms(x_ref[...]) * g_ref[0]) * (1.0 + sc) + sh
    p = _dot(h.astype(BF16), w_ref[0])
    proj_ref[...] = p[:, 0:PROJ_KEEP]
    glu = p[:, PROJ_KEEP:PROJ_KEEP + MIX_CONV] * jax.nn.sigmoid(p[:, PROJ_KEEP + MIX_CONV:PROJ_COLS])
    for lt in range(ntile):
        for r in range(nrows):
            base = r * stride + CONV_PAD
            pad_ref[lt, base:base + rlen, :] = glu[r * rlen:(r + 1) * rlen, lt * LANES:(lt + 1) * LANES]


def _inproj_conv(x, modp, norm_g, w_in_b, conv_w, conv_b, ln_g, ln_b, layer, bsz, t, rlen):
    tb = max(rlen, min(t, 512))
    nrows = tb // rlen
    nt = t // tb
    last = bsz * nt - 1
    ntile = MIX_CONV // LANES
    cur = lambda s: jnp.minimum(s, last)
    prev = lambda s: jnp.maximum(s - 1, 0)
    vec = pl.BlockSpec((1, 1, MIX_CONV), lambda s: (layer, 0, 0))
    return pl.pallas_call(
        functools.partial(_inproj_conv_kernel, nrows=nrows, rlen=rlen),
        grid=(bsz * nt + 1,),
        in_specs=[
            pl.BlockSpec((tb, D_MODEL), lambda s: (cur(s), 0)),
            pl.BlockSpec((1, 1, 6 * D_MODEL), lambda s: (cur(s) // nt, 0, 0)),
            pl.BlockSpec((1, 1, D_MODEL), lambda s: (layer, 0, 0)),
            pl.BlockSpec((1, D_MODEL, PROJ_COLS), lambda s: (layer, 0, 0)),
            pl.BlockSpec((1, ntile, CONV_W, SUBLANES, LANES), lambda s: (layer, 0, 0, 0, 0)),
            pl.BlockSpec((1, ntile, SUBLANES, LANES), lambda s: (layer, 0, 0, 0)),
            vec, vec,
        ],
        out_specs=[
            pl.BlockSpec((tb, PROJ_KEEP), lambda s: (cur(s), 0)),
            pl.BlockSpec((tb, MIX_CONV), lambda s: (prev(s), 0)),
        ],
        out_shape=[
            jax.ShapeDtypeStruct((bsz * t, PROJ_KEEP), F32),
            jax.ShapeDtypeStruct((bsz * t, MIX_CONV), BF16),
        ],
        scratch_shapes=[
            pltpu.VMEM((ntile, nrows * (rlen + 2 * CONV_PAD), LANES), F32),
            pltpu.VMEM((ntile, tb // SUBLANES, SUBLANES, LANES), F32),
        ],
        compiler_params=_cparams(),
        name="inproj_conv",
    )(x, modp, norm_g, w_in_b, conv_w, conv_b, ln_g, ln_b)


def _log_sigmoid(x):
    return jnp.minimum(x, 0.0) - jnp.log(1.0 + jnp.exp(-jnp.abs(x)))


def _head_stack(x, lane_head):
    return jnp.concatenate([jnp.where(lane_head == h, x, 0.0) for h in range(GLA_HEADS)], axis=0).astype(BF16)


def _gla_group(chains, lane_head, s_ref, rhs_ref):
    cums = []
    for qk, v, g, tri, causal, d, last_row in chains:
        g_hi = g.astype(BF16)
        g_lo = (g - g_hi.astype(F32)).astype(BF16)
        cums.append(_dot(tri, g_hi) + _dot(tri, g_lo))
    pre = []
    for (qk, v, g, tri, causal, d, last_row), cum in zip(chains, cums):
        cl = cum[last_row:last_row + 1, :]
        q = qk[:, 0:GLA_KEY]
        k = qk[:, GLA_KEY:2 * GLA_KEY]
        qd = (q * (jnp.exp(cum) * (GLA_DK ** -0.5))).astype(BF16)
        k_inv = k * jnp.exp(-cum)
        k_end = k_inv * jnp.exp(cl)
        tr = jnp.concatenate([k_end, cum], axis=0).T
        pre.append((qd, _head_stack(k_inv, lane_head), tr, v.astype(BF16)))
    mm = []
    for (qk, v, g, tri, causal, d, last_row), (qd, ks, tr, vb) in zip(chains, pre):
        sc = lax.dot_general(qd, ks, (((1,), (1,)), ((), ())), preferred_element_type=F32)
        upd = [_dot(tr[h * GLA_DK:(h + 1) * GLA_DK, 0:CHUNK].astype(BF16),
                    vb[:, h * GLA_DV:(h + 1) * GLA_DV]) for h in range(GLA_HEADS)]
        mm.append((sc, jnp.concatenate(upd, axis=0)))
    state = {}
    lhs = []
    for i, ((qk, v, g, tri, causal, d, last_row), (qd, ks, tr, vb), (sc, upd)) in enumerate(zip(chains, pre, mm)):
        s_prev = state[d] if d in state else s_ref[d]
        s_prev_b = s_prev.astype(BF16)
        for h in range(GLA_HEADS):
            rows = slice(h * GLA_DK, (h + 1) * GLA_DK)
            cols = slice(h * GLA_DV, (h + 1) * GLA_DV)
            rhs_ref[i, rows, cols] = vb[:, cols]
            rhs_ref[i, GLA_KEY + h * GLA_DK:GLA_KEY + (h + 1) * GLA_DK, cols] = s_prev_b[rows, :]
        decay = jnp.exp(tr[:, CHUNK + last_row:CHUNK + last_row + 1])
        state[d] = decay * s_prev + upd
        lhs.append(jnp.concatenate([(sc * causal).astype(BF16), qd], axis=1))
    outs = [_dot(l, rhs_ref[i]) for i, l in enumerate(lhs)]
    for d, s_new in state.items():
        s_ref[d] = s_new
    return outs


def _gla_kernel(*refs, t, zero_init, emit_state):
    qk_ref, v_ref, z_ref, wf_ref, bf_ref, wb_ref, bb_ref = refs[:7]
    pos = 7
    if not zero_init:
        s0f_ref, s0b_ref = refs[pos:pos + 2]
        pos += 2
    o_ref = refs[pos]
    pos += 1
    if emit_state:
        sf_ref, sb_ref = refs[pos:pos + 2]
        pos += 2
    gf_s, gb_s, s_ref, mask_s, rhs_s = refs[pos:pos + 5]
    rhs_s[...] = jnp.zeros(rhs_s.shape, BF16)

    z = z_ref[...].astype(BF16)
    gf_s[...] = _log_sigmoid(_dot(z, wf_ref[0]) + bf_ref[0]) * (1.0 / GATE_TAU)
    gb_s[...] = _log_sigmoid(_dot(z, wb_ref[0]) + bb_ref[0]) * (1.0 / GATE_TAU)
    if zero_init:
        s_ref[...] = jnp.zeros(s_ref.shape, F32)
    else:
        s_ref[0] = s0f_ref[0, 0].reshape(GLA_KEY, GLA_DV)
        s_ref[1] = s0b_ref[0, 0].reshape(GLA_KEY, GLA_DV)

    ri = lax.broadcasted_iota(I32, (CHUNK, CHUNK), 0)
    ci = lax.broadcasted_iota(I32, (CHUNK, CHUNK), 1)
    tri_low = jnp.where(ri >= ci, 1.0, 0.0).astype(BF16)
    tri_upp = jnp.where(ri <= ci, 1.0, 0.0).astype(BF16)
    rs = lax.broadcasted_iota(I32, (CHUNK, GLA_HEADS * CHUNK), 0)
    cs = lax.broadcasted_iota(I32, (CHUNK, GLA_HEADS * CHUNK), 1) % CHUNK
    mask_s[0] = jnp.where(rs >= cs, 1.0, 0.0)
    mask_s[1] = jnp.where(rs <= cs, 1.0, 0.0)
    lane_head = lax.broadcasted_iota(I32, (CHUNK, GLA_KEY), 1) // GLA_DK
    n = t // CHUNK

    group = min(GLA_GROUP, n // 2)

    def step(i, first):
        rows = []
        chains = []
        for j in range(group):
            c = i * group + j
            rf = pl.multiple_of(c * CHUNK, CHUNK)
            rb = pl.multiple_of((n - 1 - c) * CHUNK, CHUNK)
            rows += [rf, rb]
            chains.append((qk_ref[pl.ds(rf, CHUNK), :], v_ref[pl.ds(rf, CHUNK), :],
                           gf_s[pl.ds(rf, CHUNK), :], tri_low, mask_s[0], 0, CHUNK - 1))
            chains.append((qk_ref[pl.ds(rb, CHUNK), :], v_ref[pl.ds(rb, CHUNK), :],
                           gb_s[pl.ds(rb, CHUNK), :], tri_upp, mask_s[1], 1, 0))
        outs = _gla_group(chains, lane_head, s_ref, rhs_s)
        for r, o in zip(rows, outs):
            if first:
                o_ref[pl.ds(r, CHUNK), :] = o
            else:
                o_ref[pl.ds(r, CHUNK), :] += o

    def first_half(i, c):
        step(i, True)
        return c

    def second_half(i, c):
        step(i, False)
        return c

    half = n // (2 * group)
    lax.fori_loop(0, half, first_half, 0)
    lax.fori_loop(half, 2 * half, second_half, 0)
    if emit_state:
        sf_ref[0, 0] = s_ref[0].reshape(GLA_HEADS, GLA_DK, GLA_DV)
        sb_ref[0, 0] = s_ref[1].reshape(GLA_HEADS, GLA_DK, GLA_DV)


def _gla(proj, wdf, bdf, wdb, bdb, s0f, s0b, layer, bsz, t, emit_state):
    zero_init = s0f is None
    st_block = (1, 1, GLA_HEADS, GLA_DK, GLA_DV)
    in_specs = [
        pl.BlockSpec((t, COLBLK), lambda b: (b, 0)),
        pl.BlockSpec((t, COLBLK), lambda b: (b, 1)),
        pl.BlockSpec((t, LANES), lambda b: (b, Z_COLBLK)),
        pl.BlockSpec((1, LANES, GLA_KEY), lambda b: (layer, 0, 0)),
        pl.BlockSpec((1, 1, GLA_KEY), lambda b: (layer, 0, 0)),
        pl.BlockSpec((1, LANES, GLA_KEY), lambda b: (layer, 0, 0)),
        pl.BlockSpec((1, 1, GLA_KEY), lambda b: (layer, 0, 0)),
    ]
    args = [proj, proj, proj, wdf, bdf, wdb, bdb]
    if not zero_init:
        in_specs += [pl.BlockSpec(st_block, lambda b: (b, layer, 0, 0, 0))] * 2
        args += [s0f, s0b]
    out_specs = [pl.BlockSpec((t, MIX_GLA), lambda b: (b, 0))]
    out_shape = [jax.ShapeDtypeStruct((bsz * t, MIX_GLA), F32)]
    if emit_state:
        out_specs += [pl.BlockSpec((1, 1, GLA_HEADS, GLA_DK, GLA_DV), lambda b: (b, 0, 0, 0, 0))] * 2
        out_shape += [jax.ShapeDtypeStruct((bsz, 1, GLA_HEADS, GLA_DK, GLA_DV), F32)] * 2
    return pl.pallas_call(
        functools.partial(_gla_kernel, t=t, zero_init=zero_init, emit_state=emit_state),
        grid=(bsz,),
        in_specs=in_specs,
        out_specs=out_specs,
        out_shape=out_shape,
        scratch_shapes=[
            pltpu.VMEM((t, GLA_KEY), F32),
            pltpu.VMEM((t, GLA_KEY), F32),
            pltpu.VMEM((2, GLA_KEY, GLA_DV), F32),
            pltpu.VMEM((2, CHUNK, GLA_HEADS * CHUNK), F32),
            pltpu.VMEM((2 * min(GLA_GROUP, t // CHUNK // 2), 2 * GLA_KEY, MIX_GLA), BF16),
        ],
        compiler_params=_cparams(),
        name="gla",
    )(*args)


def _outproj_kernel(o_ref, r_ref, u_ref, x_ref, mod_ref, gn_ref, wo_ref, g2_ref, wr_ref,
                    xo_ref, h2_ref, aff_ref):
    gn = gn_ref[0]
    m = mod_ref[0]
    gt1 = m[:, 2 * D_MODEL:3 * D_MODEL]
    sh2 = m[:, 3 * D_MODEL:4 * D_MODEL]
    sc2 = m[:, 4 * D_MODEL:5 * D_MODEL]
    tm = o_ref.shape[0]
    subs = [slice(r0, r0 + OUTPROJ_SUB) for r0 in range(0, tm, OUTPROJ_SUB)]
    acts = []
    for rs in subs:
        o = o_ref[rs, :]
        r = r_ref[rs, :]
        parts = []
        for h in range(GLA_HEADS):
            hs = slice(h * GLA_DV, (h + 1) * GLA_DV)
            parts.append(((_rms(o[:, hs]) * gn) * _silu(r[:, hs])).astype(BF16))
        parts.append(u_ref[rs, :])
        acts.append(jnp.concatenate(parts, axis=1))
    mixed = [_dot(a, wo_ref[0]) for a in acts]
    h2s = []
    for rs, mix in zip(subs, mixed):
        x = x_ref[rs, :] + gt1 * mix
        xo_ref[rs, :] = x
        h2 = ((_rms(x) * g2_ref[0]) * (1.0 + sc2) + sh2).astype(BF16)
        h2_ref[rs, :] = h2
        h2s.append(h2)
    logit = [_dot(h2, wr_ref[0]) for h2 in h2s]
    for rs, logits in zip(subs, logit):
        lane = lax.broadcasted_iota(I32, logits.shape, 1)
        logits = jnp.where(lane < N_EXPERTS, logits, -jnp.inf)
        ex = jnp.exp(logits - jnp.max(logits, axis=-1, keepdims=True))
        aff_ref[rs, :] = ex / jnp.sum(ex, axis=-1, keepdims=True)


def _outproj(o, proj, u, x, modp, gla_norm_g, w_out_b, norm_ffn_g, w_router_b, layer, bsz, t):
    tm = min(t, 512)
    nt = t // tm
    row = lambda b, i: (b * nt + i, 0)
    return pl.pallas_call(
        _outproj_kernel,
        grid=(bsz, nt),
        in_specs=[
            pl.BlockSpec((tm, MIX_GLA), row),
            pl.BlockSpec((tm, COLBLK), lambda b, i: (b * nt + i, 2)),
            pl.BlockSpec((tm, MIX_CONV), row),
            pl.BlockSpec((tm, D_MODEL), row),
            pl.BlockSpec((1, 1, 6 * D_MODEL), lambda b, i: (b, 0, 0)),
            pl.BlockSpec((1, 1, GLA_DV), lambda b, i: (layer, 0, 0)),
            pl.BlockSpec((1, D_MODEL, D_MODEL), lambda b, i: (layer, 0, 0)),
            pl.BlockSpec((1, 1, D_MODEL), lambda b, i: (layer, 0, 0)),
            pl.BlockSpec((1, D_MODEL, LANES), lambda b, i: (layer, 0, 0)),
        ],
        out_specs=[
            pl.BlockSpec((tm, D_MODEL), row),
            pl.BlockSpec((tm, D_MODEL), row),
            pl.BlockSpec((tm, LANES), row),
        ],
        out_shape=[
            jax.ShapeDtypeStruct((bsz * t, D_MODEL), F32),
            jax.ShapeDtypeStruct((bsz * t, D_MODEL), BF16),
            jax.ShapeDtypeStruct((bsz * t, LANES), F32),
        ],
        compiler_params=_cparams(),
        name="outproj",
    )(o, proj, u, x, modp, gla_norm_g, w_out_b, norm_ffn_g, w_router_b)


PREFIX_BLK = 256


def _excl_prefix(x, t):
    blk = min(t, PREFIX_BLK)
    ri = lax.broadcasted_iota(I32, (blk, blk), 0)
    ci = lax.broadcasted_iota(I32, (blk, blk), 1)
    upper = jnp.where(ri < ci, 1.0, 0.0).astype(BF16)
    run = jnp.zeros((x.shape[0], 1), F32)
    outs = []
    for j in range(t // blk):
        xb = x[:, j * blk:(j + 1) * blk]
        outs.append(_dot(xb.astype(BF16), upper) + run)
        run = run + jnp.sum(xb, axis=1, keepdims=True)
    return jnp.concatenate(outs, axis=1) if len(outs) > 1 else outs[0]


F32_MAGNITUDE_BITS = 31
REFINE_STEPS = 16


def _route_kernel(aff_ref, sel_ref, selt_ref, afft_ref, at_s, *, bsz, t, cap):
    for b in range(bsz):
        at_s[b * N_EXPERTS:(b + 1) * N_EXPERTS, :] = aff_ref[b * t:(b + 1) * t, :].T[0:N_EXPERTS, :]
    a = at_s[...]
    afft_ref[...] = a
    rows = bsz * N_EXPERTS
    capf = float(cap)

    def enough(th):
        return jnp.sum(jnp.where(a >= th, 1.0, 0.0), axis=1, keepdims=True) >= capf

    def bit_step(i, thr):
        cand = thr | jnp.left_shift(jnp.int32(1), F32_MAGNITUDE_BITS - 1 - i)
        return jnp.where(enough(lax.bitcast_convert_type(cand, F32)), cand, thr)

    thr = lax.fori_loop(0, F32_MAGNITUDE_BITS, bit_step, jnp.zeros((rows, 1), I32))
    lo = lax.bitcast_convert_type(thr, F32)
    hi = lax.bitcast_convert_type(thr + 1, F32)

    def refine(i, c):
        lo, hi = c
        mid = lo + (hi - lo) * 0.5
        ok = enough(mid)
        return jnp.where(ok, mid, lo), jnp.where(ok, hi, mid)

    lo, hi = lax.fori_loop(0, REFINE_STEPS, refine, (lo, hi))
    gt = jnp.where(a >= hi, 1.0, 0.0) * jnp.where(a > lo, 1.0, 0.0)
    eq = jnp.where(a >= lo, 1.0, 0.0) - gt
    need = capf - jnp.sum(gt, axis=1, keepdims=True)
    tie_rank = _excl_prefix(eq, t)
    m = gt + eq * jnp.where(tie_rank < need, 1.0, 0.0)
    slot = _excl_prefix(m, t)
    sel = jnp.where(m > 0.5, slot.astype(I32), -1)
    sel_ref[...] = sel
    filler = jnp.full((LANES - N_EXPERTS, t), -1, I32)
    for b in range(bsz):
        selt_ref[b * t:(b + 1) * t, :] = jnp.concatenate(
            [sel[b * N_EXPERTS:(b + 1) * N_EXPERTS, :], filler], axis=0).T


def _route(aff, bsz, t, cap):
    rows = bsz * N_EXPERTS
    sel, selt, afft = pl.pallas_call(
        functools.partial(_route_kernel, bsz=bsz, t=t, cap=cap),
        grid=(1,),
        in_specs=[pl.BlockSpec((bsz * t, LANES), lambda i: (0, 0))],
        out_specs=[
            pl.BlockSpec((rows, t), lambda i: (0, 0)),
            pl.BlockSpec((bsz * t, LANES), lambda i: (0, 0)),
            pl.BlockSpec((rows, t), lambda i: (0, 0)),
        ],
        out_shape=[
            jax.ShapeDtypeStruct((rows, t), I32),
            jax.ShapeDtypeStruct((bsz * t, LANES), I32),
            jax.ShapeDtypeStruct((rows, t), F32),
        ],
        scratch_shapes=[pltpu.VMEM((rows, t), F32)],
        compiler_params=_cparams(),
        name="route",
    )(aff)
    return sel.reshape(bsz, N_EXPERTS, t), selt, afft.reshape(bsz, N_EXPERTS, t)


def _gather_kernel(h_ref, sel_ref, afft_ref, xs_ref, gate_ref, *, t, cap, eg):
    g = pl.program_id(1)
    hb = h_ref[...]
    slot = lax.broadcasted_iota(I32, (cap, t), 0)
    for el in range(eg):
        if eg == N_EXPERTS:
            srow = sel_ref[0, el:el + 1, :]
            arow = afft_ref[0, el:el + 1, :]
        else:
            e = g * eg + el
            srow = sel_ref[0, pl.ds(e, 1), :]
            arow = afft_ref[0, pl.ds(e, 1), :]
        hit = srow == slot
        p = jnp.where(hit, 1.0, 0.0).astype(BF16)
        xs_ref[el, 0] = _dot(p, hb).astype(BF16)
        gate_ref[el, 0] = jnp.sum(jnp.where(hit, arow, 0.0), axis=1, keepdims=True)


GATHER_SLOTS = 64


def _gather_win_kernel(h_ref, sel_ref, afft_ref, xs_ref, gate_ref, p_s, *, t, cap):
    nsub = cap // GATHER_SLOTS
    per = t // nsub
    win = 2 * per
    j = pl.program_id(1)
    s0 = j * GATHER_SLOTS
    lo = pl.multiple_of(jnp.clip(j * per - per // 2, 0, t - win), per // 2)
    sel = sel_ref[0]
    rel = sel - s0
    tok = lax.broadcasted_iota(I32, (1, t), 1)
    mine = jnp.where(rel < 0, 0.0, jnp.where(rel >= GATHER_SLOTS, 0.0, 1.0))
    outside = mine * jnp.where(tok < lo, 1.0, jnp.where(tok >= lo + win, 1.0, 0.0))
    windowed = jnp.max(outside) == 0.0
    slot = lax.broadcasted_iota(I32, (GATHER_SLOTS, 1), 0)

    def run(width, sel_w, aff_w, h_w):
        for e in range(N_EXPERTS):
            hit = (sel_w[e:e + 1, :] - s0) == slot
            p_s[e * GATHER_SLOTS:(e + 1) * GATHER_SLOTS, 0:width] = jnp.where(hit, 1.0, 0.0).astype(BF16)
            gate_ref[e, 0] = jnp.sum(jnp.where(hit, aff_w[e:e + 1, :], 0.0), axis=1, keepdims=True)
        xs = _dot(p_s[:, 0:width], h_w).astype(BF16)
        xs_ref[:, 0] = xs.reshape(N_EXPERTS, GATHER_SLOTS, D_MODEL)

    @pl.when(windowed)
    def _():
        run(win, sel_ref[0, :, pl.ds(lo, win)], afft_ref[0, :, pl.ds(lo, win)], h_ref[pl.ds(lo, win), :])

    @pl.when(jnp.logical_not(windowed))
    def _():
        run(t, sel, afft_ref[0], h_ref[...])


def _gather(h2, sel, afft, bsz, t, cap):
    if cap % GATHER_SLOTS == 0 and cap // GATHER_SLOTS > 1:
        return pl.pallas_call(
            functools.partial(_gather_win_kernel, t=t, cap=cap),
            grid=(bsz, cap // GATHER_SLOTS),
            in_specs=[
                pl.BlockSpec((t, D_MODEL), lambda b, j: (b, 0)),
                pl.BlockSpec((1, N_EXPERTS, t), lambda b, j: (b, 0, 0)),
                pl.BlockSpec((1, N_EXPERTS, t), lambda b, j: (b, 0, 0)),
            ],
            out_specs=[
                pl.BlockSpec((N_EXPERTS, 1, GATHER_SLOTS, D_MODEL), lambda b, j: (0, b, j, 0)),
                pl.BlockSpec((N_EXPERTS, 1, GATHER_SLOTS, 1), lambda b, j: (0, b, j, 0)),
            ],
            out_shape=[
                jax.ShapeDtypeStruct((N_EXPERTS, bsz, cap, D_MODEL), BF16),
                jax.ShapeDtypeStruct((N_EXPERTS, bsz, cap, 1), F32),
            ],
            scratch_shapes=[pltpu.VMEM((N_EXPERTS * GATHER_SLOTS, t), BF16)],
            compiler_params=_cparams(),
            name="gather",
        )(h2, sel, afft)
    eg = max(1, min(N_EXPERTS, 512 // cap))
    return pl.pallas_call(
        functools.partial(_gather_kernel, t=t, cap=cap, eg=eg),
        grid=(bsz, N_EXPERTS // eg),
        in_specs=[
            pl.BlockSpec((t, D_MODEL), lambda b, g: (b, 0)),
            pl.BlockSpec((1, N_EXPERTS, t), lambda b, g: (b, 0, 0)),
            pl.BlockSpec((1, N_EXPERTS, t), lambda b, g: (b, 0, 0)),
        ],
        out_specs=[
            pl.BlockSpec((eg, 1, cap, D_MODEL), lambda b, g: (g, b, 0, 0)),
            pl.BlockSpec((eg, 1, cap, 1), lambda b, g: (g, b, 0, 0)),
        ],
        out_shape=[
            jax.ShapeDtypeStruct((N_EXPERTS, bsz, cap, D_MODEL), BF16),
            jax.ShapeDtypeStruct((N_EXPERTS, bsz, cap, 1), F32),
        ],
        compiler_params=_cparams(),
        name="gather",
    )(h2, sel, afft)


FFN_TF = 512
FFN_TM = 512


def _ffn_kernel(xc_ref, xl_ref, gc_ref, gl_ref, wg_ref, wu_ref, wd_ref, yc_ref, yl_ref, acc_ref):
    def step(first):
        wg = wg_ref[0, 0].astype(BF16)
        wu = wu_ref[0, 0].astype(BF16)
        wd = wd_ref[0, 0].astype(BF16)
        base = 0
        for x_ref, g_ref, y_ref in ((xc_ref, gc_ref, yc_ref), (xl_ref, gl_ref, yl_ref)):
            rows = x_ref.shape[1]
            for r0 in range(0, rows, FFN_TM):
                x = x_ref[0, r0:r0 + FFN_TM, :]
                hm = (_silu(_dot(x, wg)) * _dot(x, wu)).astype(BF16)
                part = _dot(hm, wd)
                if first:
                    acc_ref[base + r0:base + r0 + FFN_TM, :] = part
                else:
                    acc = acc_ref[base + r0:base + r0 + FFN_TM, :] + part
                    acc_ref[base + r0:base + r0 + FFN_TM, :] = acc
                    y_ref[0, r0:r0 + FFN_TM, :] = (acc * g_ref[0, r0:r0 + FFN_TM, :]).astype(BF16)
            base += rows

    j = pl.program_id(1)
    pl.when(j == 0)(functools.partial(step, True))
    pl.when(j > 0)(functools.partial(step, False))


def _ffn(xs_c, gates_c, xs_l, gates_l, w_e_gate, w_e_up, w_e_down, layer):
    rc = xs_c.shape[1]
    rl = xs_l.shape[1]
    assert rc % FFN_TM == 0 and rl % FFN_TM == 0
    rowblk = lambda r, w: pl.BlockSpec((1, r, w), lambda e, j: (e, 0, 0))
    return pl.pallas_call(
        _ffn_kernel,
        grid=(N_EXPERTS, D_EXPERT // FFN_TF),
        in_specs=[
            rowblk(rc, D_MODEL), rowblk(rl, D_MODEL), rowblk(rc, 1), rowblk(rl, 1),
            pl.BlockSpec((1, 1, D_MODEL, FFN_TF), lambda e, j: (layer, e, 0, j)),
            pl.BlockSpec((1, 1, D_MODEL, FFN_TF), lambda e, j: (layer, e, 0, j)),
            pl.BlockSpec((1, 1, FFN_TF, D_MODEL), lambda e, j: (layer, e, j, 0)),
        ],
        out_specs=[rowblk(rc, D_MODEL), rowblk(rl, D_MODEL)],
        out_shape=[
            jax.ShapeDtypeStruct((N_EXPERTS, rc, D_MODEL), BF16),
            jax.ShapeDtypeStruct((N_EXPERTS, rl, D_MODEL), BF16),
        ],
        scratch_shapes=[pltpu.VMEM((rc + rl, D_MODEL), F32)],
        compiler_params=_cparams(),
        name="ffn",
    )(xs_c, xs_l, gates_c, gates_l, w_e_gate, w_e_up, w_e_down)


def _scatter_kernel(*refs, cap, nt, final):
    if final:
        selt_ref, y_ref, x_ref, mod_ref, gf_ref, xo_ref, pt_s, yw_s = refs
    else:
        selt_ref, y_ref, x_ref, mod_ref, xo_ref, pt_s, yw_s = refs
    tt = selt_ref.shape[0]
    ncol = N_EXPERTS * cap
    gw = min(ncol, 1024)
    gt2 = mod_ref[0][:, 5 * D_MODEL:6 * D_MODEL]

    def finish(moe):
        x = x_ref[...] + gt2 * moe
        if final:
            x = _rms(x) * gf_ref[...]
        xo_ref[...] = x

    if nt > 1:
        win = 2 * cap // nt
        st = selt_ref[...]
        first = jnp.min(jnp.where(st < 0, cap, st), axis=0, keepdims=True)
        lo = jnp.minimum(first & -BF16_ROWS, cap - win)
        rel = st - lo
        outside = jnp.where(st < 0, 0.0, jnp.where(rel >= win, 1.0, 0.0))
        windowed = jnp.max(outside) == 0.0

        @pl.when(windowed)
        def _():
            slot = lax.broadcasted_iota(I32, (1, win), 1)
            for e in range(N_EXPERTS):
                pt_s[:, e * win:(e + 1) * win] = jnp.where(rel[:, e:e + 1] == slot, 1.0, 0.0).astype(BF16)
                lo_e = pl.multiple_of(lo[0, e], BF16_ROWS)
                yw_s[e * win:(e + 1) * win, :] = y_ref[e, 0, pl.ds(lo_e, win), :]
            finish(_dot(pt_s[:, 0:N_EXPERTS * win], yw_s[...]))

        @pl.when(jnp.logical_not(windowed))
        def _():
            slot = lax.broadcasted_iota(I32, (1, cap), 1)
            for e in range(N_EXPERTS):
                pt_s[:, e * cap:(e + 1) * cap] = jnp.where(st[:, e:e + 1] == slot, 1.0, 0.0).astype(BF16)
            finish(_dot(pt_s[...], y_ref[:, 0].reshape(ncol, D_MODEL)))

        return
    if cap % LANES == 0:
        st = selt_ref[...]
        slot = lax.broadcasted_iota(I32, (1, cap), 1)
        for e in range(N_EXPERTS):
            pt_s[:, e * cap:(e + 1) * cap] = jnp.where(st[:, e:e + 1] == slot, 1.0, 0.0).astype(BF16)
    else:
        stf = selt_ref[...].astype(F32).astype(BF16)
        col = lax.broadcasted_iota(I32, (LANES, gw), 1)
        row = lax.broadcasted_iota(I32, (LANES, gw), 0)
        cmod = (lax.broadcasted_iota(I32, (1, gw), 1) % cap).astype(F32)
        for g in range(ncol // gw):
            expand = jnp.where((col + g * gw) // cap == row, 1.0, 0.0).astype(BF16)
            selx = _dot(stf, expand)
            pt_s[:, g * gw:(g + 1) * gw] = jnp.where(selx == cmod, 1.0, 0.0).astype(BF16)
    finish(_dot(pt_s[...], y_ref[:, 0].reshape(ncol, D_MODEL)))


def _scatter(selt, y, x, modp, norm_final_g, bsz, t, cap):
    final = norm_final_g is not None
    tt = min(t, 256)
    nt = t // tt
    row = lambda b, i: (b * nt + i, 0)
    in_specs = [
        pl.BlockSpec((tt, LANES), row),
        pl.BlockSpec((N_EXPERTS, 1, cap, D_MODEL), lambda b, i: (0, b, 0, 0)),
        pl.BlockSpec((tt, D_MODEL), row),
        pl.BlockSpec((1, 1, 6 * D_MODEL), lambda b, i: (b, 0, 0)),
    ]
    args = [selt, y, x, modp]
    if final:
        in_specs.append(pl.BlockSpec((1, D_MODEL), lambda b, i: (0, 0)))
        args.append(norm_final_g)
    return pl.pallas_call(
        functools.partial(_scatter_kernel, cap=cap, nt=nt, final=final),
        grid=(bsz, nt),
        in_specs=in_specs,
        out_specs=pl.BlockSpec((tt, D_MODEL), row),
        out_shape=jax.ShapeDtypeStruct((bsz * t, D_MODEL), F32),
        scratch_shapes=[
            pltpu.VMEM((tt, N_EXPERTS * cap), BF16),
            pltpu.VMEM((N_EXPERTS * 2 * cap // nt, D_MODEL) if nt > 1 else (2 * SUBLANES, LANES), BF16),
        ],
        compiler_params=_cparams(),
        name="scatter",
    )(*args)


def _mixer_and_route(x, modp, s0f, s0b, wts, layer, bsz, t, rlen, emit_state):
    cap = EC_FACTOR * t // N_EXPERTS
    proj, u = _inproj_conv(x, modp, wts["norm_mix_g"], wts["w_in"], wts["conv_w"], wts["conv_b"],
                           wts["conv_ln_g"], wts["conv_ln_b"], layer, bsz, t, rlen)
    gla = _gla(proj, wts["wdf"], wts["bdf"], wts["wdb"], wts["bdb"], s0f, s0b, layer, bsz, t, emit_state)
    if emit_state:
        o, sf, sb = gla
    else:
        (o,) = gla
        sf = sb = None
    x, h2, aff = _outproj(o, proj, u, x, modp, wts["gla_norm_g"], wts["w_out"], wts["norm_ffn_g"],
                          wts["w_router"], layer, bsz, t)
    sel, selt, afft = _route(aff, bsz, t, cap)
    xs, gates = _gather(h2, sel, afft, bsz, t, cap)
    return (x, selt, xs.reshape(N_EXPERTS, bsz * cap, D_MODEL), gates.reshape(N_EXPERTS, bsz * cap, 1)), sf, sb


def kernel(x_prompt, x_sample, state_gla_fwd, state_gla_bwd, c, c_ctx, norm_mix_g, norm_ffn_g, norm_final_g, w_mod, b_mod, w_in, w_decay_f, b_decay_f, w_decay_b, b_decay_b, gla_norm_g, conv_w, conv_b, conv_ln_g, conv_ln_b, w_out, w_router, w_e_gate, w_e_up, w_e_down):
    depth = w_in.shape[0]
    bp, tp, _ = x_prompt.shape
    bs, ts, _ = x_sample.shape

    zcols = 2 * GLA_KEY + 2 * MIX_GLA
    zend = zcols + 2 * GATE_RANK
    w_in_r = jnp.zeros((depth, D_MODEL, PROJ_COLS), BF16)
    w_in_r = lax.dynamic_update_slice(w_in_r, w_in[:, :, :zend].astype(BF16), (0, 0, 0))
    w_in_r = lax.dynamic_update_slice(w_in_r, w_in[:, :, zend:].astype(BF16), (0, 0, PROJ_KEEP))
    zpad_f = jnp.zeros((depth, LANES - GATE_RANK, GLA_KEY), F32)
    zpad_b0 = jnp.zeros((depth, GATE_RANK, GLA_KEY), F32)
    zpad_b1 = jnp.zeros((depth, LANES - 2 * GATE_RANK, GLA_KEY), F32)
    vec = lambda a: a.reshape(depth, 1, a.shape[-1])
    wts = dict(
        norm_mix_g=vec(norm_mix_g), norm_ffn_g=vec(norm_ffn_g), w_in=w_in_r,
        wdf=jnp.concatenate([w_decay_f, zpad_f], axis=1).astype(BF16), bdf=vec(b_decay_f),
        wdb=jnp.concatenate([zpad_b0, w_decay_b, zpad_b1], axis=1).astype(BF16), bdb=vec(b_decay_b),
        gla_norm_g=vec(gla_norm_g),
        conv_w=jnp.broadcast_to(
            conv_w.reshape(depth, CONV_W, 1, MIX_CONV // LANES, LANES).transpose(0, 3, 1, 2, 4),
            (depth, MIX_CONV // LANES, CONV_W, SUBLANES, LANES)),
        conv_b=jnp.broadcast_to(conv_b.reshape(depth, MIX_CONV // LANES, 1, LANES),
                                (depth, MIX_CONV // LANES, SUBLANES, LANES)),
        conv_ln_g=vec(conv_ln_g), conv_ln_b=vec(conv_ln_b), w_out=w_out.astype(BF16),
        w_router=jnp.concatenate(
            [w_router, jnp.zeros((depth, D_MODEL, LANES - N_EXPERTS), F32)], axis=2).astype(BF16),
        w_e_gate=w_e_gate, w_e_up=w_e_up, w_e_down=w_e_down,
    )

    nrow = 2 * SUBLANES
    cond = jnp.concatenate([c, c_ctx[None, :], jnp.zeros((nrow - bs - 1, D_MODEL), F32)], axis=0)
    mod = _modulation(cond, w_mod, b_mod)

    xp = x_prompt.reshape(bp * tp, D_MODEL)
    xs = x_sample.reshape(bs * ts, D_MODEL)
    final_g = norm_final_g.reshape(1, D_MODEL)
    fwd_states = []
    bwd_states = []
    for l in range(depth):
        last = l == depth - 1
        mod_ctx = jnp.broadcast_to(mod[l, bs][None, None, :], (bp, 1, 6 * D_MODEL))
        mod_lat = mod[l, :bs][:, None, :]
        (xp, selt_c, xs_c, gates_c), sf, sb = _mixer_and_route(
            xp, mod_ctx, None, None, wts, l, bp, tp, tp, True)
        fwd_states.append(sf)
        bwd_states.append(sb)
        (xs, selt_l, xs_l, gates_l), _, _ = _mixer_and_route(
            xs, mod_lat, state_gla_fwd, state_gla_bwd, wts, l, bs, ts, GRID_W, False)
        y_c, y_l = _ffn(xs_c, gates_c, xs_l, gates_l, wts["w_e_gate"], wts["w_e_up"], wts["w_e_down"], l)
        cap_c = EC_FACTOR * tp // N_EXPERTS
        cap_l = EC_FACTOR * ts // N_EXPERTS
        fin = final_g if last else None
        xp = _scatter(selt_c, y_c.reshape(N_EXPERTS, bp, cap_c, D_MODEL), xp, mod_ctx, fin, bp, tp, cap_c)
        xs = _scatter(selt_l, y_l.reshape(N_EXPERTS, bs, cap_l, D_MODEL), xs, mod_lat, fin, bs, ts, cap_l)
    y_prompt = xp.reshape(bp, tp, D_MODEL)
    y_sample = xs.reshape(bs, ts, D_MODEL)
    return (y_prompt, y_sample, jnp.concatenate(fwd_states, axis=1), jnp.concatenate(bwd_states, axis=1))
```
